```python
import jax, jax.numpy as jnp
from jax import lax
import numpy as np

D_MODEL = 1024
BATCH = 4
SEQ = 4096
DEPTH = 1
DEC_BATCH = 16
DEC_SEQ = 32
PAST_LEN = 4096

CHUNK = 64
N_HEADS = 8
HEAD_DIM = 64
D_ATTN = N_HEADS * HEAD_DIM
D_CONV = D_MODEL // 2
CONV_WIDTH = 31
D_FF = ((8 * D_MODEL // 3 + 255) // 256) * 256
Q_BLOCK = 128
RMS_EPS = 1e-6
LN_EPS = 1e-5
NEG = -1e30
D_IN = 2 * D_CONV + 3 * D_ATTN + N_HEADS + 2 * D_MODEL

kernel_name = "hybrid_conformer_fox_stream_step"


def rmsnorm(x, g):
    xf = x.astype(jnp.float32)
    y = xf * lax.rsqrt(jnp.mean(xf * xf, axis=-1, keepdims=True) + RMS_EPS)
    return (y * g.astype(jnp.float32)).astype(x.dtype)


def layernorm(x, g, b):
    xf = x.astype(jnp.float32)
    mu = jnp.mean(xf, axis=-1, keepdims=True)
    var = jnp.mean(jnp.square(xf - mu), axis=-1, keepdims=True)
    y = (xf - mu) * lax.rsqrt(var + LN_EPS)
    return (y * g.astype(jnp.float32) + b.astype(jnp.float32)).astype(x.dtype)


def depthwise_causal(xpad, w_dw, b_dw):
    out = lax.conv_general_dilated(
        xpad, w_dw[:, None, :].astype(xpad.dtype), window_strides=(1,), padding='VALID',
        dimension_numbers=('NWC', 'WIO', 'NWC'), feature_group_count=D_CONV)
    return out + b_dw.astype(out.dtype)


def fox_block(q, k, v, cq, ck, q_pos, k_pos):
    s = jnp.einsum('bqhd,bkhd->bhqk', q.astype(jnp.float32), k.astype(jnp.float32)) * (HEAD_DIM ** -0.5)
    decay = jnp.transpose(cq, (0, 2, 1))[:, :, :, None] - jnp.transpose(ck, (0, 2, 1))[:, :, None, :]
    mask = k_pos[None, :] <= q_pos[:, None]
    s = jnp.where(mask[None, None], s + decay, NEG)
    p = jax.nn.softmax(s, axis=-1)
    return jnp.einsum('bhqk,bkhd->bqhd', p, v.astype(jnp.float32)).astype(v.dtype)


def fox_prompt(q, k, v, logf):
    B, S = q.shape[0], q.shape[1]
    c = jnp.cumsum(logf, axis=1)
    pos = jnp.arange(S)
    nb = S // Q_BLOCK
    qb = jnp.transpose(q.reshape(B, nb, Q_BLOCK, N_HEADS, HEAD_DIM), (1, 0, 2, 3, 4))
    cb = jnp.transpose(c.reshape(B, nb, Q_BLOCK, N_HEADS), (1, 0, 2, 3))
    pb = pos.reshape(nb, Q_BLOCK)
    out = lax.map(lambda a: fox_block(a[0], k, v, a[1], c, a[2], pos), (qb, cb, pb))
    return jnp.transpose(out, (1, 0, 2, 3, 4)).reshape(B, S, N_HEADS, HEAD_DIM)


def fox_sample(q, k, v, logf, cache_k, cache_v, cache_logf):
    P, T = cache_k.shape[1], q.shape[1]
    kk = jnp.concatenate([cache_k.astype(k.dtype), k], axis=1)
    vv = jnp.concatenate([cache_v.astype(v.dtype), v], axis=1)
    c = jnp.cumsum(jnp.concatenate([cache_logf.astype(jnp.float32), logf], axis=1), axis=1)
    k_pos = jnp.arange(P + T)
    q_pos = P + jnp.arange(T)
    return fox_block(q, kk, vv, c[:, P:], c, q_pos, k_pos)


def layer(x, conv_hist, attend, norm_mix_g, w_in, b_f, w_dw, b_dw, ln_g, ln_b,
          w_conv_pw, w_attn_o, w_out, norm_ffn_g, w_gate, w_up, w_down):
    B, T, _ = x.shape
    u = rmsnorm(x, norm_mix_g) @ w_in
    o = 2 * D_CONV
    glu_in = u[..., :o]
    q = u[..., o:o + D_ATTN].reshape(B, T, N_HEADS, HEAD_DIM); o += D_ATTN
    k = u[..., o:o + D_ATTN].reshape(B, T, N_HEADS, HEAD_DIM); o += D_ATTN
    v = u[..., o:o + D_ATTN].reshape(B, T, N_HEADS, HEAD_DIM); o += D_ATTN
    f_logit = u[..., o:o + N_HEADS]; o += N_HEADS
    g_conv = u[..., o:o + D_MODEL]; o += D_MODEL
    g_attn = u[..., o:o + D_MODEL]
    logf = jax.nn.log_sigmoid(f_logit.astype(jnp.float32) + b_f.astype(jnp.float32))

    a, gl = jnp.split(glu_in, 2, axis=-1)
    cu = a * jax.nn.sigmoid(gl)
    cpad = jnp.concatenate([conv_hist.astype(cu.dtype), cu], axis=1)
    new_hist = cpad[:, -(CONV_WIDTH - 1):]
    cy = jax.nn.silu(layernorm(depthwise_causal(cpad, w_dw, b_dw), ln_g, ln_b)) @ w_conv_pw

    ao = attend(q, k, v, logf).reshape(B, T, D_ATTN) @ w_attn_o

    mixed = jax.nn.sigmoid(g_conv) * cy + jax.nn.sigmoid(g_attn) * ao
    h = x + mixed @ w_out

    z = rmsnorm(h, norm_ffn_g)
    h = h + (jax.nn.silu(z @ w_gate) * (z @ w_up)) @ w_down
    return h, k, v, logf.astype(x.dtype), new_hist


def setup_inputs(seed: int = 0) -> dict:
    key = jax.random.key(seed)
    ks = jax.random.split(key, 24)
    L = DEPTH

    def nrm(k, shape, scale):
        return jax.random.normal(k, shape, jnp.float32) * scale

    return {
        "x_prompt": nrm(ks[0], (BATCH, SEQ, D_MODEL), 1.0),
        "x_sample": nrm(ks[1], (DEC_BATCH, DEC_SEQ, D_MODEL), 1.0),
        "cache_k": nrm(ks[2], (L, DEC_BATCH, PAST_LEN, N_HEADS, HEAD_DIM), 1.0),
        "cache_v": nrm(ks[3], (L, DEC_BATCH, PAST_LEN, N_HEADS, HEAD_DIM), 1.0),
        "cache_logf": jax.nn.log_sigmoid(nrm(ks[4], (L, DEC_BATCH, PAST_LEN, N_HEADS), 1.0) + 3.0),
        "state_conv": nrm(ks[5], (L, DEC_BATCH, CONV_WIDTH - 1, D_CONV), 1.0),
        "norm_mix_g": 1.0 + nrm(ks[6], (L, D_MODEL), 0.02),
        "w_in": nrm(ks[7], (L, D_MODEL, D_IN), D_MODEL ** -0.5),
        "b_f": jax.random.uniform(ks[8], (L, N_HEADS), jnp.float32, 1.0, 5.0),
        "w_dw": nrm(ks[9], (L, CONV_WIDTH, D_CONV), CONV_WIDTH ** -0.5),
        "b_dw": nrm(ks[10], (L, D_CONV), 0.02),
        "ln_g": 1.0 + nrm(ks[11], (L, D_CONV), 0.02),
        "ln_b": nrm(ks[12], (L, D_CONV), 0.02),
        "w_conv_pw": nrm(ks[13], (L, D_CONV, D_MODEL), D_CONV ** -0.5),
        "w_attn_o": nrm(ks[14], (L, D_ATTN, D_MODEL), D_ATTN ** -0.5),
        "w_out": nrm(ks[15], (L, D_MODEL, D_MODEL), D_MODEL ** -0.5),
        "norm_ffn_g": 1.0 + nrm(ks[16], (L, D_MODEL), 0.02),
        "w_gate": nrm(ks[17], (L, D_MODEL, D_FF), D_MODEL ** -0.5),
        "w_up": nrm(ks[18], (L, D_MODEL, D_FF), D_MODEL ** -0.5),
        "w_down": nrm(ks[19], (L, D_FF, D_MODEL), D_FF ** -0.5),
        "final_norm_g": 1.0 + nrm(ks[20], (D_MODEL,), 0.02),
    }


def reference(x_prompt, x_sample, cache_k, cache_v, cache_logf, state_conv,
              norm_mix_g, w_in, b_f, w_dw, b_dw, ln_g, ln_b, w_conv_pw, w_attn_o, w_out,
              norm_ffn_g, w_gate, w_up, w_down, final_norm_g):
    hp, hs = x_prompt, x_sample
    kp_l, vp_l, fp_l, cp_l = [], [], [], []
    ks_l, vs_l, fs_l, cs_l = [], [], [], []
    for l in range(DEPTH):
        wts = (norm_mix_g[l], w_in[l], b_f[l], w_dw[l], b_dw[l], ln_g[l], ln_b[l],
               w_conv_pw[l], w_attn_o[l], w_out[l], norm_ffn_g[l], w_gate[l], w_up[l], w_down[l])
        zero_hist = jnp.zeros((hp.shape[0], CONV_WIDTH - 1, D_CONV), hp.dtype)
        hp, kp, vp, fp, cp = layer(hp, zero_hist, fox_prompt, *wts)
        ck, cv, cf = cache_k[l], cache_v[l], cache_logf[l]
        attend_s = lambda q, k, v, lf: fox_sample(q, k, v, lf, ck, cv, cf)
        hs, ksn, vsn, fsn, csn = layer(hs, state_conv[l], attend_s, *wts)
        kp_l.append(kp); vp_l.append(vp); fp_l.append(fp); cp_l.append(cp)
        ks_l.append(ksn); vs_l.append(vsn); fs_l.append(fsn); cs_l.append(csn)
    y_prompt = rmsnorm(hp, final_norm_g)
    y_sample = rmsnorm(hs, final_norm_g)
    return (y_prompt, y_sample,
            jnp.stack(kp_l), jnp.stack(vp_l), jnp.stack(fp_l), jnp.stack(cp_l),
            jnp.stack(ks_l), jnp.stack(vs_l), jnp.stack(fs_l), jnp.stack(cs_l))
```

```python
import functools

import jax
import jax.numpy as jnp
import numpy as np
from jax import lax
from jax.experimental import pallas as pl
from jax.experimental.pallas import tpu as pltpu

D_MODEL = 1024
N_HEADS = 8
HEAD_DIM = 64
D_ATTN = N_HEADS * HEAD_DIM
D_CONV = D_MODEL // 2
CONV_WIDTH = 31
HIST = CONV_WIDTH - 1
D_FF = 2816
RMS_EPS = 1e-6
LN_EPS = 1e-5
NEG = -1e30
SCALE = HEAD_DIM ** -0.5

LANES = 128
HIST_PAD = 32
CONV_ROWS = 32
N_EXTRA = 6
VMEM_LIMIT = 56 * 1024 * 1024

F32 = jnp.float32
BF16 = jnp.bfloat16


def _dot(a, b):
    return jnp.dot(a, b, preferred_element_type=F32)


def _dot_nt(a, b):
    return lax.dot_general(a, b, (((1,), (1,)), ((), ())), preferred_element_type=F32)


def _const_spec(shape):
    return pl.BlockSpec(shape, lambda *_: (0,) * len(shape), pipeline_mode=pl.Buffered(1))


def _split3(c):
    hi = c.astype(BF16)
    r1 = c - hi.astype(F32)
    mid = r1.astype(BF16)
    lo = (r1 - mid.astype(F32)).astype(BF16)
    return hi, mid, lo


def _inproj_kernel(x_ref, hist_ref, g_ref, wglu_ref, wqkv_ref, wf_ref, bf_ref, wgc_ref, wga_ref,
                   wdw_ref, bdw_ref, lng_ref, lnb_ref, wpw_ref, pq_ref, pk_ref, cq_ref, ck_ref,
                   q_ref, qx_ref, kx_ref, k_ref, v_ref, lf_ref, c_ref, sga_ref, cyg_ref, nh_ref,
                   cpad_ref, act_ref, ccar_ref, *, nb, T, carry):
    R = nb * T
    t = pl.program_id(1)

    xf = x_ref[...]
    ms = jnp.mean(xf * xf, axis=-1, keepdims=True)
    xn = (xf * lax.rsqrt(ms + RMS_EPS) * g_ref[...]).astype(BF16)

    glu = _dot(xn, wglu_ref[...])
    cu = glu[:, :D_CONV] * jax.nn.sigmoid(glu[:, D_CONV:])
    if carry:
        @pl.when(t == 0)
        def _():
            cpad_ref[:, 0:HIST_PAD, :] = hist_ref[...]
            ccar_ref[...] = jnp.zeros_like(ccar_ref)
    else:
        cpad_ref[:, 0:HIST_PAD, :] = hist_ref[...]
    for b in range(nb):
        cpad_ref[b, HIST_PAD:HIST_PAD + T, :] = cu[b * T:(b + 1) * T]

    off = HIST_PAD - HIST
    for b in range(nb):
        for r0 in range(0, T, CONV_ROWS):
            acc = jnp.broadcast_to(bdw_ref[...], (CONV_ROWS, D_CONV))
            for w in range(CONV_WIDTH):
                s0 = off + w + r0
                acc = acc + cpad_ref[b, s0:s0 + CONV_ROWS, :] * wdw_ref[w:w + 1, :]
            mu = jnp.mean(acc, axis=-1, keepdims=True)
            dv = acc - mu
            var = jnp.mean(dv * dv, axis=-1, keepdims=True)
            y = dv * lax.rsqrt(var + LN_EPS) * lng_ref[...] + lnb_ref[...]
            act_ref[b * T + r0:b * T + r0 + CONV_ROWS, :] = (y * jax.nn.sigmoid(y)).astype(BF16)

    nh_ref[...] = cpad_ref[:, T:T + HIST_PAD, :]
    if carry:
        cpad_ref[:, 0:HIST_PAD, :] = cpad_ref[:, T:T + HIST_PAD, :]

    cy = _dot(act_ref[...], wpw_ref[...])
    cyg_ref[...] = (jax.nn.sigmoid(_dot(xn, wgc_ref[...])) * cy).astype(BF16)
    sga_ref[...] = jax.nn.sigmoid(_dot(xn, wga_ref[...])).astype(BF16)

    qkv = _dot(xn, wqkv_ref[...])
    q_ref[...] = (qkv[:, :D_ATTN] * SCALE).astype(BF16)
    k_ref[...] = qkv[:, D_ATTN:2 * D_ATTN]
    v_ref[...] = qkv[:, 2 * D_ATTN:]

    z = _dot(xn, wf_ref[...]) + bf_ref[...]
    lf = -(jnp.maximum(-z, 0.0) + jnp.log1p(jnp.exp(-jnp.abs(z))))
    lane = lax.broadcasted_iota(jnp.int32, (R, LANES), 1)
    lf = jnp.where(lane < N_HEADS, lf, 0.0)
    lf_ref[...] = lf
    row = lax.broadcasted_iota(jnp.int32, (R, LANES), 0) % T
    c = lf
    d = 1
    while d < T:
        c = c + jnp.where(row >= d, pltpu.roll(c, d, axis=0), 0.0)
        d *= 2
    if carry:
        c = c + ccar_ref[0:1, :]
        ccar_ref[...] = jnp.broadcast_to(c[R - 1:R, :], ccar_ref.shape)
    c_ref[...] = c

    hi, mid, lo = _split3(c)
    qx = _dot(hi, pq_ref[0]) + _dot(mid, pq_ref[1]) + _dot(lo, pq_ref[2]) + cq_ref[...]
    kx = _dot(hi, pk_ref[0]) + _dot(mid, pk_ref[1]) + _dot(lo, pk_ref[2]) + ck_ref[...]
    qx_ref[...] = qx.astype(BF16)
    kx_ref[...] = kx.astype(BF16)


def _extra_lane(h):
    pair, odd = divmod(h, 2)
    return pair * LANES + (0 if odd else HEAD_DIM)


def _placement_constants():
    pq = np.zeros((3, LANES, D_ATTN), np.float32)
    pk = np.zeros((3, LANES, D_ATTN), np.float32)
    cq = np.zeros((1, D_ATTN), np.float32)
    ck = np.zeros((1, D_ATTN), np.float32)
    for h in range(N_HEADS):
        p = _extra_lane(h)
        for part in range(3):
            pq[part, h, p + part] = 1.0
            pk[part, h, p + 3 + part] = -1.0
            cq[0, p + 3 + part] = 1.0
            ck[0, p + part] = 1.0
    return pq, pk, cq, ck


def _inproj(x2d, hist, w, *, nb, T, carry):
    n_rows = x2d.shape[0]
    R = nb * T
    n_seg = hist.shape[0]
    n_outer = n_seg // nb
    n_t = n_rows // (R * n_outer)
    grid = (n_outer, n_t)
    row_map = lambda b, t: (b * n_t + t, 0)
    pq, pk, cq, ck = _placement_constants()

    def rows(width, dtype):
        return jax.ShapeDtypeStruct((n_rows, width), dtype), pl.BlockSpec((R, width), row_map)

    outs = [rows(D_ATTN, BF16), rows(D_ATTN, BF16), rows(D_ATTN, BF16), rows(D_ATTN, F32),
            rows(D_ATTN, F32), rows(LANES, F32), rows(LANES, F32), rows(D_MODEL, BF16),
            rows(D_MODEL, BF16)]
    out_shape = [o[0] for o in outs] + [jax.ShapeDtypeStruct((n_seg, HIST_PAD, D_CONV), F32)]
    out_specs = [o[1] for o in outs] + [pl.BlockSpec((nb, HIST_PAD, D_CONV), lambda b, t: (b, 0, 0))]

    consts = [w["g1"], w["wglu"], w["wqkv"], w["wf"], w["bf"], w["wgc"], w["wga"], w["wdw"],
              w["bdw"], w["lng"], w["lnb"], w["wpw"], jnp.asarray(pq, BF16), jnp.asarray(pk, BF16),
              jnp.asarray(cq), jnp.asarray(ck)]
    in_specs = [pl.BlockSpec((R, D_MODEL), row_map),
                pl.BlockSpec((nb, HIST_PAD, D_CONV), lambda b, t: (b, 0, 0))]
    in_specs += [_const_spec(c.shape) for c in consts]

    return pl.pallas_call(
        functools.partial(_inproj_kernel, nb=nb, T=T, carry=carry),
        grid=grid,
        in_specs=in_specs,
        out_specs=out_specs,
        out_shape=out_shape,
        scratch_shapes=[pltpu.VMEM((nb, T + HIST_PAD, D_CONV), F32),
                        pltpu.VMEM((R, D_CONV), BF16),
                        pltpu.VMEM((8, LANES), F32)],
        compiler_params=pltpu.CompilerParams(
            dimension_semantics=("arbitrary", "arbitrary"), vmem_limit_bytes=VMEM_LIMIT),
        name="inproj_conv",
    )(x2d, hist, *consts)


def _attn_prompt_kernel(q_ref, qx_ref, k_ref, kx_ref, v_ref, o_ref, ka_ref, va_ref, *, tq):
    i = pl.program_id(2)
    lane = lax.broadcasted_iota(jnp.int32, (tq, LANES), 1)
    low = lane < HEAD_DIM
    row0 = pl.multiple_of(i * tq, tq)

    q = q_ref[0].astype(F32)
    qx = qx_ref[0].astype(F32)
    k = k_ref[0]
    kx = kx_ref[0].astype(F32)
    v = v_ref[0]
    one_hi = (lane == HEAD_DIM).astype(F32)
    one_lo = (lane == 0).astype(F32)
    ka_ref[pl.ds(row0, tq), 0:LANES] = jnp.where(low, k, kx).astype(BF16)
    ka_ref[pl.ds(row0, tq), LANES:2 * LANES] = jnp.where(low, kx, k).astype(BF16)
    va_ref[pl.ds(row0, tq), 0:LANES] = jnp.where(low, v, one_hi).astype(BF16)
    va_ref[pl.ds(row0, tq), LANES:2 * LANES] = jnp.where(low, one_lo, v).astype(BF16)
    qa = (jnp.where(low, q, qx).astype(BF16), jnp.where(low, qx, q).astype(BF16))

    def step(j, carry, masked):
        col0 = pl.multiple_of(j * tq, tq)
        out = []
        for e in range(2):
            m, acc = carry[e]
            kj = ka_ref[pl.ds(col0, tq), e * LANES:(e + 1) * LANES]
            vj = va_ref[pl.ds(col0, tq), e * LANES:(e + 1) * LANES]
            s = _dot_nt(qa[e], kj)
            if masked:
                r_id = lax.broadcasted_iota(jnp.int32, (tq, tq), 0)
                c_id = lax.broadcasted_iota(jnp.int32, (tq, tq), 1)
                s = jnp.where(c_id <= r_id, s, NEG)
            m_new = jnp.maximum(m, jnp.max(s, axis=-1, keepdims=True))
            alpha = jnp.exp(m - m_new)
            p = jnp.exp(s - m_new).astype(BF16)
            out.append((m_new, alpha * acc + _dot(p, vj)))
        return tuple(out)

    init = tuple((jnp.full((tq, 1), NEG, F32), jnp.zeros((tq, LANES), F32)) for _ in range(2))
    carry = lax.fori_loop(0, i, lambda j, c: step(j, c, False), init)
    (_, acc0), (_, acc1) = step(i, carry, True)
    o0 = acc0 / acc0[:, HEAD_DIM:HEAD_DIM + 1]
    o1 = acc1 / acc1[:, 0:1]
    o_ref[0] = jnp.where(low, o0, o1).astype(BF16)


def _attn_prompt(q, qx, k, kx, v, *, batch, seq, tq):
    shp = (batch, seq, D_ATTN)
    q, qx, k, kx, v = (a.reshape(shp) for a in (q, qx, k, kx, v))
    spec = pl.BlockSpec((1, tq, LANES), lambda b, hp, i: (b, i, hp))
    return pl.pallas_call(
        functools.partial(_attn_prompt_kernel, tq=tq),
        grid=(batch, N_HEADS // 2, seq // tq),
        in_specs=[spec] * 5,
        out_specs=spec,
        out_shape=jax.ShapeDtypeStruct(shp, BF16),
        scratch_shapes=[pltpu.VMEM((seq, 2 * LANES), BF16), pltpu.VMEM((seq, 2 * LANES), BF16)],
        compiler_params=pltpu.CompilerParams(
            dimension_semantics=("arbitrary", "arbitrary", "arbitrary"),
            vmem_limit_bytes=VMEM_LIMIT),
        name="attn_prompt",
    )(q, qx, k, kx, v).reshape(batch * seq, D_ATTN)


def _attn_sample_kernel(qbd_ref, ck_ref, cv_ref, clf_ref, kn_ref, vn_ref, rn_ref, o_ref,
                        m_ref, acc_ref, car_ref, *, tk, nq):
    j = pl.program_id(1)
    n_rows = N_HEADS * nq
    qbd = qbd_ref[0]

    def expand(r, n):
        return jnp.concatenate(
            [jnp.broadcast_to(r[h:h + 1, :], (nq, n)) for h in range(N_HEADS)], axis=0)

    def update(s, vb, m, acc):
        m_new = jnp.maximum(m, jnp.max(s, axis=-1, keepdims=True))
        alpha = jnp.exp(m - m_new)
        p = jnp.exp(s - m_new).astype(BF16)
        return m_new, alpha * acc + _dot(p, vb)

    @pl.when(j == 0)
    def _():
        s = _dot_nt(qbd, kn_ref[0].astype(BF16)) + expand(rn_ref[0], nq)
        q_id = lax.broadcasted_iota(jnp.int32, (n_rows, nq), 0) % nq
        k_id = lax.broadcasted_iota(jnp.int32, (n_rows, nq), 1)
        s = jnp.where(k_id <= q_id, s, NEG)
        ones = jnp.ones((nq, LANES), BF16)
        vb = jnp.concatenate([vn_ref[0].astype(BF16), ones], axis=1)
        m0 = jnp.full((n_rows, 1), NEG, F32)
        m_new, acc = update(s, vb, m0, jnp.zeros((n_rows, D_ATTN + LANES), F32))
        m_ref[...] = jnp.broadcast_to(m_new, m_ref.shape)
        acc_ref[...] = acc
        car_ref[...] = jnp.zeros_like(car_ref)

    lf = clf_ref[0]
    lane = lax.broadcasted_iota(jnp.int32, (N_HEADS, tk), 1)
    x = lf
    d = 1
    while d < tk:
        x = x + jnp.where(lane < tk - d, pltpu.roll(x, tk - d, axis=1), 0.0)
        d *= 2
    r = x - lf + car_ref[:, 0:1]
    car_ref[...] = car_ref[...] + x[:, 0:1]

    s = _dot_nt(qbd, ck_ref[0].astype(BF16)) + expand(r, tk)
    ones = jnp.ones((tk, LANES), BF16)
    vb = jnp.concatenate([cv_ref[0].astype(BF16), ones], axis=1)
    m_new, acc = update(s, vb, m_ref[:, 0:1], acc_ref[...])
    m_ref[...] = jnp.broadcast_to(m_new, m_ref.shape)
    acc_ref[...] = acc

    @pl.when(j == pl.num_programs(1) - 1)
    def _():
        a = acc_ref[...]
        o = a[:, :D_ATTN] / a[:, D_ATTN:D_ATTN + 1]
        grp = lax.broadcasted_iota(jnp.int32, (nq, D_ATTN), 1) // HEAD_DIM
        out = jnp.zeros((nq, D_ATTN), F32)
        for h in range(N_HEADS):
            out = jnp.where(grp == h, o[h * nq:(h + 1) * nq, :], out)
        o_ref[0] = out.astype(BF16)


def _attn_sample(qbd, cache_k, cache_v, clf_t, k_new, v_new, r_new, *, tk):
    nb, past, _ = cache_k.shape
    nq = k_new.shape[1]
    nk = past // tk
    n_rows = N_HEADS * nq
    return pl.pallas_call(
        functools.partial(_attn_sample_kernel, tk=tk, nq=nq),
        grid=(nb, nk),
        in_specs=[pl.BlockSpec((1, n_rows, D_ATTN), lambda b, j: (b, 0, 0)),
                  pl.BlockSpec((1, tk, D_ATTN), lambda b, j: (b, nk - 1 - j, 0)),
                  pl.BlockSpec((1, tk, D_ATTN), lambda b, j: (b, nk - 1 - j, 0)),
                  pl.BlockSpec((1, N_HEADS, tk), lambda b, j: (b, 0, nk - 1 - j)),
                  pl.BlockSpec((1, nq, D_ATTN), lambda b, j: (b, 0, 0)),
                  pl.BlockSpec((1, nq, D_ATTN), lambda b, j: (b, 0, 0)),
                  pl.BlockSpec((1, N_HEADS, nq), lambda b, j: (b, 0, 0))],
        out_specs=pl.BlockSpec((1, nq, D_ATTN), lambda b, j: (b, 0, 0)),
        out_shape=jax.ShapeDtypeStruct((nb, nq, D_ATTN), BF16),
        scratch_shapes=[pltpu.VMEM((n_rows, LANES), F32),
                        pltpu.VMEM((n_rows, D_ATTN + LANES), F32),
                        pltpu.VMEM((N_HEADS, LANES), F32)],
        compiler_params=pltpu.CompilerParams(
            dimension_semantics=("arbitrary", "arbitrary"), vmem_limit_bytes=VMEM_LIMIT),
        name="attn_sample",
    )(qbd, cache_k, cache_v, clf_t, k_new, v_new, r_new)


def _rms(x, g):
    return x * lax.rsqrt(jnp.mean(x * x, axis=-1, keepdims=True) + RMS_EPS) * g


def _out_ffn_kernel(x_ref, ao_ref, sga_ref, cyg_ref, wao_ref, wout_ref, g2_ref, wg_ref, wu_ref,
                    wd_ref, gf_ref, y_ref):
    ap = _dot(ao_ref[...], wao_ref[...])
    mixed = cyg_ref[...].astype(F32) + sga_ref[...].astype(F32) * ap
    h = x_ref[...] + _dot(mixed.astype(BF16), wout_ref[...])
    z = _rms(h, g2_ref[...]).astype(BF16)
    gate = _dot(z, wg_ref[...])
    act = (gate * jax.nn.sigmoid(gate) * _dot(z, wu_ref[...])).astype(BF16)
    h = h + _dot(act, wd_ref[...])
    y_ref[...] = _rms(h, gf_ref[...])


def _out_ffn(x2d, ao, sga, cyg, w, *, R):
    n_rows = x2d.shape[0]
    row_map = lambda i: (i, 0)
    consts = [w["wao"], w["wout"], w["g2"], w["wg"], w["wu"], w["wd"], w["gf"]]
    in_specs = [pl.BlockSpec((R, D_MODEL), row_map), pl.BlockSpec((R, D_ATTN), row_map),
                pl.BlockSpec((R, D_MODEL), row_map), pl.BlockSpec((R, D_MODEL), row_map)]
    in_specs += [_const_spec(c.shape) for c in consts]
    return pl.pallas_call(
        _out_ffn_kernel,
        grid=(n_rows // R,),
        in_specs=in_specs,
        out_specs=pl.BlockSpec((R, D_MODEL), row_map),
        out_shape=jax.ShapeDtypeStruct((n_rows, D_MODEL), F32),
        compiler_params=pltpu.CompilerParams(
            dimension_semantics=("arbitrary",), vmem_limit_bytes=VMEM_LIMIT),
        name="out_ffn",
    )(x2d, ao, sga, cyg, *consts)


def _prep_weights(norm_mix_g, w_in, b_f, w_dw, b_dw, ln_g, ln_b, w_conv_pw, w_attn_o, w_out,
                  norm_ffn_g, w_gate, w_up, w_down, final_norm_g):
    o_q = 2 * D_CONV
    o_f = o_q + 3 * D_ATTN
    o_gc = o_f + N_HEADS
    o_ga = o_gc + D_MODEL
    row = lambda a: a.reshape(1, -1).astype(F32)
    return {
        "g1": row(norm_mix_g),
        "wglu": w_in[:, :o_q].astype(BF16),
        "wqkv": w_in[:, o_q:o_f].astype(BF16),
        "wf": jnp.pad(w_in[:, o_f:o_gc], ((0, 0), (0, LANES - N_HEADS))).astype(BF16),
        "bf": jnp.pad(row(b_f), ((0, 0), (0, LANES - N_HEADS))),
        "wgc": w_in[:, o_gc:o_ga].astype(BF16),
        "wga": w_in[:, o_ga:].astype(BF16),
        "wdw": jnp.pad(w_dw.astype(F32), ((0, HIST_PAD - CONV_WIDTH), (0, 0))),
        "bdw": row(b_dw), "lng": row(ln_g), "lnb": row(ln_b),
        "wpw": w_conv_pw.astype(BF16),
        "wao": w_attn_o.astype(BF16), "wout": w_out.astype(BF16), "g2": row(norm_ffn_g),
        "wg": w_gate.astype(BF16), "wu": w_up.astype(BF16), "wd": w_down.astype(BF16),
        "gf": row(final_norm_g),
    }


def kernel(x_prompt, x_sample, cache_k, cache_v, cache_logf, state_conv, norm_mix_g, w_in, b_f,
           w_dw, b_dw, ln_g, ln_b, w_conv_pw, w_attn_o, w_out, norm_ffn_g, w_gate, w_up, w_down,
           final_norm_g):
    B, S, _ = x_prompt.shape
    NB, T, _ = x_sample.shape
    P = cache_k.shape[2]
    w = _prep_weights(norm_mix_g[0], w_in[0], b_f[0], w_dw[0], b_dw[0], ln_g[0], ln_b[0],
                      w_conv_pw[0], w_attn_o[0], w_out[0], norm_ffn_g[0], w_gate[0], w_up[0],
                      w_down[0], final_norm_g)

    xp = x_prompt.reshape(B * S, D_MODEL)
    hist0 = jnp.zeros((B, HIST_PAD, D_CONV), F32)
    q, qx, kx, k, v, lf, _, sga, cyg, nh = _inproj(xp, hist0, w, nb=1, T=256, carry=True)
    ao = _attn_prompt(q, qx, k, kx, v, batch=B, seq=S, tq=512)
    y_prompt = _out_ffn(xp, ao, sga, cyg, w, R=256).reshape(B, S, D_MODEL)
    k_prompt = k.reshape(1, B, S, N_HEADS, HEAD_DIM)
    v_prompt = v.reshape(1, B, S, N_HEADS, HEAD_DIM)
    logf_prompt = lf[:, :N_HEADS].reshape(1, B, S, N_HEADS)
    conv_prompt = nh[:, HIST_PAD - HIST:, :].reshape(1, B, HIST, D_CONV)

    xs = x_sample.reshape(NB * T, D_MODEL)
    hist_s = jnp.pad(state_conv[0].astype(F32), ((0, 0), (HIST_PAD - HIST, 0), (0, 0)))
    q, _, _, k, v, lf, c, sga, cyg, nh = _inproj(xs, hist_s, w, nb=NB, T=T, carry=False)
    q3 = jnp.tile(q.reshape(NB, 1, T, D_ATTN), (1, N_HEADS, 1, 1))
    head_of_row = jnp.arange(N_HEADS)[:, None, None]
    head_of_col = (jnp.arange(D_ATTN) // HEAD_DIM)[None, None, :]
    qbd = jnp.where(head_of_row == head_of_col, q3, 0).reshape(NB, N_HEADS * T, D_ATTN)
    r_new = -jnp.transpose(c[:, :N_HEADS].reshape(NB, T, N_HEADS), (0, 2, 1))
    clf_t = jnp.transpose(cache_logf[0].astype(F32), (0, 2, 1))
    ao = _attn_sample(qbd, cache_k[0].reshape(NB, P, D_ATTN), cache_v[0].reshape(NB, P, D_ATTN),
                      clf_t, k.reshape(NB, T, D_ATTN), v.reshape(NB, T, D_ATTN), r_new, tk=2048)
    y_sample = _out_ffn(xs, ao.reshape(NB * T, D_ATTN), sga, cyg, w, R=NB * T)
    y_sample = y_sample.reshape(NB, T, D_MODEL)
    k_sample = k.reshape(1, NB, T, N_HEADS, HEAD_DIM)
    v_sample = v.reshape(1, NB, T, N_HEADS, HEAD_DIM)
    logf_sample = lf[:, :N_HEADS].reshape(1, NB, T, N_HEADS)
    conv_sample = nh[:, HIST_PAD - HIST:, :].reshape(1, NB, HIST, D_CONV)

    return (y_prompt, y_sample, k_prompt, v_prompt, logf_prompt, conv_prompt,
            k_sample, v_sample, logf_sample, conv_sample)
```

```python
import functools

import jax
import jax.numpy as jnp
import numpy as np
from jax import lax
from jax.experimental import pallas as pl
from jax.experimental.pallas import tpu as pltpu

D_MODEL = 1024
N_HEADS = 8
HEAD_DIM = 64
D_ATTN = N_HEADS * HEAD_DIM
D_CONV = D_MODEL // 2
CONV_WIDTH = 31
HIST = CONV_WIDTH - 1
D_FF = 2816
RMS_EPS = 1e-6
LN_EPS = 1e-5
NEG = -1e30
SCALE = HEAD_DIM ** -0.5

LANES = 128
HIST_PAD = 32
CONV_ROWS = 64
N_EXTRA = 6
VMEM_LIMIT = 56 * 1024 * 1024

F32 = jnp.float32
BF16 = jnp.bfloat16


def _dot(a, b):
    return jnp.dot(a, b, preferred_element_type=F32)


def _dot_nt(a, b):
    return lax.dot_general(a, b, (((1,), (1,)), ((), ())), preferred_element_type=F32)


def _const_spec(shape):
    return pl.BlockSpec(shape, lambda *_: (0,) * len(shape), pipeline_mode=pl.Buffered(1))


def _split3(c):
    hi = c.astype(BF16)
    r1 = c - hi.astype(F32)
    mid = r1.astype(BF16)
    lo = (r1 - mid.astype(F32)).astype(BF16)
    return hi, mid, lo


def _inproj_kernel(x_ref, hist_ref, g_ref, wglu_ref, wqkv_ref, wf_ref, bf_ref, wgc_ref, wga_ref,
                   wdw_ref, bdw_ref, lng_ref, lnb_ref, wpw_ref, pq_ref, pk_ref, cq_ref, ck_ref,
                   q_ref, qx_ref, kx_ref, kb_ref, vb_ref, k5_ref, v5_ref, lf_ref, c_ref, sga_ref,
                   cyg_ref, nh_ref, cpad_ref, act_ref, ccar_ref, *, nb, T, carry):
    R = nb * T
    RB = min(T, CONV_ROWS)
    t = pl.program_id(1)

    xf = x_ref[...]
    ms = jnp.mean(xf * xf, axis=-1, keepdims=True)
    xn = (xf * lax.rsqrt(ms + RMS_EPS) * g_ref[...]).astype(BF16)

    glu = _dot(xn, wglu_ref[...])
    cu = glu[:, :D_CONV] * jax.nn.sigmoid(glu[:, D_CONV:])
    if carry:
        @pl.when(t == 0)
        def _():
            cpad_ref[:, 0:HIST_PAD, :] = hist_ref[...]
            ccar_ref[...] = jnp.zeros_like(ccar_ref)
    else:
        cpad_ref[:, 0:HIST_PAD, :] = hist_ref[...]
    for b in range(nb):
        cpad_ref[b, HIST_PAD:HIST_PAD + T, :] = cu[b * T:(b + 1) * T]

    off = HIST_PAD - HIST
    win_rows = RB + HIST_PAD
    for b in range(nb):
        for r0 in range(0, T, RB):
            tiles = []
            for lt in range(D_CONV // LANES):
                ls = slice(lt * LANES, (lt + 1) * LANES)
                win = cpad_ref[b, r0:r0 + win_rows, ls]
                acc = jnp.broadcast_to(bdw_ref[:, ls], (RB, LANES))
                for res in range(8):
                    sh = win if res == 0 else pltpu.roll(win, win_rows - res, axis=0)
                    for a in range(HIST_PAD // 8 + 1):
                        w = 8 * a + res - off
                        if 0 <= w < CONV_WIDTH:
                            acc = acc + sh[8 * a:8 * a + RB] * wdw_ref[w:w + 1, ls]
                tiles.append(acc)
            acc = jnp.concatenate(tiles, axis=1)
            mu = jnp.mean(acc, axis=-1, keepdims=True)
            dv = acc - mu
            var = jnp.mean(dv * dv, axis=-1, keepdims=True)
            y = dv * lax.rsqrt(var + LN_EPS) * lng_ref[...] + lnb_ref[...]
            act_ref[b * T + r0:b * T + r0 + RB, :] = (y * jax.nn.sigmoid(y)).astype(BF16)

    nh_ref[...] = cpad_ref[:, T:T + HIST_PAD, :]
    if carry:
        cpad_ref[:, 0:HIST_PAD, :] = cpad_ref[:, T:T + HIST_PAD, :]

    cy = _dot(act_ref[...], wpw_ref[...])
    cyg_ref[...] = (jax.nn.sigmoid(_dot(xn, wgc_ref[...])) * cy).astype(BF16)
    sga_ref[...] = jax.nn.sigmoid(_dot(xn, wga_ref[...])).astype(BF16)

    qkv = _dot(xn, wqkv_ref[...])
    q_ref[...] = (qkv[:, :D_ATTN] * SCALE).astype(BF16)
    kf = qkv[:, D_ATTN:2 * D_ATTN]
    vf = qkv[:, 2 * D_ATTN:]
    kb_ref[...] = kf.astype(BF16)
    vb_ref[...] = vf.astype(BF16)
    for h in range(N_HEADS):
        hs = slice(h * HEAD_DIM, (h + 1) * HEAD_DIM)
        k5_ref[pl.ds(h, R, stride=N_HEADS), :] = kf[:, hs]
        v5_ref[pl.ds(h, R, stride=N_HEADS), :] = vf[:, hs]

    z = _dot(xn, wf_ref[...]) + bf_ref[...]
    lf = -(jnp.maximum(-z, 0.0) + jnp.log1p(jnp.exp(-jnp.abs(z))))
    lane = lax.broadcasted_iota(jnp.int32, (R, LANES), 1)
    lf = jnp.where(lane < N_HEADS, lf, 0.0)
    lf_ref[...] = lf
    row = lax.broadcasted_iota(jnp.int32, (R, LANES), 0) % T
    c = lf
    d = 1
    while d < T:
        c = c + jnp.where(row >= d, pltpu.roll(c, d, axis=0), 0.0)
        d *= 2
    if carry:
        c = c + ccar_ref[0:1, :]
        ccar_ref[...] = jnp.broadcast_to(c[R - 1:R, :], ccar_ref.shape)
    c_ref[...] = c

    hi, mid, lo = _split3(c)
    qx = _dot(hi, pq_ref[0]) + _dot(mid, pq_ref[1]) + _dot(lo, pq_ref[2]) + cq_ref[...]
    kx = _dot(hi, pk_ref[0]) + _dot(mid, pk_ref[1]) + _dot(lo, pk_ref[2]) + ck_ref[...]
    qx_ref[...] = qx.astype(BF16)
    kx_ref[...] = kx.astype(BF16)


def _extra_lane(h):
    pair, odd = divmod(h, 2)
    return pair * LANES + (0 if odd else HEAD_DIM)


def _placement_constants():
    pq = np.zeros((3, LANES, D_ATTN), np.float32)
    pk = np.zeros((3, LANES, D_ATTN), np.float32)
    cq = np.zeros((1, D_ATTN), np.float32)
    ck = np.zeros((1, D_ATTN), np.float32)
    for h in range(N_HEADS):
        p = _extra_lane(h)
        for part in range(3):
            pq[part, h, p + part] = 1.0
            pk[part, h, p + 3 + part] = -1.0
            cq[0, p + 3 + part] = 1.0
            ck[0, p + part] = 1.0
    return pq, pk, cq, ck


def _inproj(x2d, hist, w, *, nb, T, carry):
    n_rows = x2d.shape[0]
    R = nb * T
    n_seg = hist.shape[0]
    n_outer = n_seg // nb
    n_t = n_rows // (R * n_outer)
    grid = (n_outer, n_t)
    row_map = lambda b, t: (b * n_t + t, 0)
    pq, pk, cq, ck = _placement_constants()

    def rows(width, dtype):
        return jax.ShapeDtypeStruct((n_rows, width), dtype), pl.BlockSpec((R, width), row_map)

    def head_rows():
        return (jax.ShapeDtypeStruct((n_rows * N_HEADS, HEAD_DIM), F32),
                pl.BlockSpec((R * N_HEADS, HEAD_DIM), row_map))

    outs = [rows(D_ATTN, BF16), rows(D_ATTN, BF16), rows(D_ATTN, BF16), rows(D_ATTN, BF16),
            rows(D_ATTN, BF16), head_rows(), head_rows(), rows(LANES, F32), rows(LANES, F32),
            rows(D_MODEL, BF16), rows(D_MODEL, BF16)]
    out_shape = [o[0] for o in outs] + [jax.ShapeDtypeStruct((n_seg, HIST_PAD, D_CONV), F32)]
    out_specs = [o[1] for o in outs] + [pl.BlockSpec((nb, HIST_PAD, D_CONV), lambda b, t: (b, 0, 0))]

    consts = [w["g1"], w["wglu"], w["wqkv"], w["wf"], w["bf"], w["wgc"], w["wga"], w["wdw"],
              w["bdw"], w["lng"], w["lnb"], w["wpw"], jnp.asarray(pq, BF16), jnp.asarray(pk, BF16),
              jnp.asarray(cq), jnp.asarray(ck)]
    in_specs = [pl.BlockSpec((R, D_MODEL), row_map),
                pl.BlockSpec((nb, HIST_PAD, D_CONV), lambda b, t: (b, 0, 0))]
    in_specs += [_const_spec(c.shape) for c in consts]

    return pl.pallas_call(
        functools.partial(_inproj_kernel, nb=nb, T=T, carry=carry),
        grid=grid,
        in_specs=in_specs,
        out_specs=out_specs,
        out_shape=out_shape,
        scratch_shapes=[pltpu.VMEM((nb, T + HIST_PAD, D_CONV), F32),
                        pltpu.VMEM((R, D_CONV), BF16),
                        pltpu.VMEM((8, LANES), F32)],
        compiler_params=pltpu.CompilerParams(
            dimension_semantics=("arbitrary", "arbitrary"), vmem_limit_bytes=VMEM_LIMIT),
        name="inproj_conv",
    )(x2d, hist, *consts)


def _attn_prompt_kernel(q_ref, qx_ref, k_ref, kx_ref, v_ref, o_ref, ka_ref, va_ref, *, tq):
    i = pl.program_id(2)
    lane = lax.broadcasted_iota(jnp.int32, (tq, LANES), 1)
    low = lane < HEAD_DIM
    row0 = pl.multiple_of(i * tq, tq)

    q = q_ref[0]
    qx = qx_ref[0]
    k = k_ref[0]
    kx = kx_ref[0]
    v = v_ref[0]
    one_hi = (lane == HEAD_DIM).astype(BF16)
    one_lo = (lane == 0).astype(BF16)
    ka_ref[pl.ds(row0, tq), 0:LANES] = jnp.where(low, k, kx)
    ka_ref[pl.ds(row0, tq), LANES:2 * LANES] = jnp.where(low, kx, k)
    va_ref[pl.ds(row0, tq), 0:LANES] = jnp.where(low, v, one_hi)
    va_ref[pl.ds(row0, tq), LANES:2 * LANES] = jnp.where(low, one_lo, v)
    qa = (jnp.where(low, q, qx), jnp.where(low, qx, q))

    def step(j, carry, masked):
        col0 = pl.multiple_of(j * tq, tq)
        out = []
        for e in range(2):
            m, acc = carry[e]
            kj = ka_ref[pl.ds(col0, tq), e * LANES:(e + 1) * LANES]
            vj = va_ref[pl.ds(col0, tq), e * LANES:(e + 1) * LANES]
            s = _dot_nt(qa[e], kj)
            if masked:
                r_id = lax.broadcasted_iota(jnp.int32, (tq, tq), 0)
                c_id = lax.broadcasted_iota(jnp.int32, (tq, tq), 1)
                s = jnp.where(c_id <= r_id, s, NEG)
            m_new = jnp.maximum(m, jnp.max(s, axis=-1, keepdims=True))
            alpha = jnp.exp(m - m_new)
            p = jnp.exp(s - m_new).astype(BF16)
            out.append((m_new, alpha * acc + _dot(p, vj)))
        return tuple(out)

    init = tuple((jnp.full((tq, 1), NEG, F32), jnp.zeros((tq, LANES), F32)) for _ in range(2))
    carry = lax.fori_loop(0, i, lambda j, c: step(j, c, False), init)
    (_, acc0), (_, acc1) = step(i, carry, True)
    o0 = acc0 / acc0[:, HEAD_DIM:HEAD_DIM + 1]
    o1 = acc1 / acc1[:, 0:1]
    o_ref[0] = jnp.where(low, o0, o1).astype(BF16)


def _attn_prompt(q, qx, k, kx, v, *, batch, seq, tq):
    shp = (batch, seq, D_ATTN)
    q, qx, k, kx, v = (a.reshape(shp) for a in (q, qx, k, kx, v))
    spec = pl.BlockSpec((1, tq, LANES), lambda b, hp, i: (b, i, hp))
    return pl.pallas_call(
        functools.partial(_attn_prompt_kernel, tq=tq),
        grid=(batch, N_HEADS // 2, seq // tq),
        in_specs=[spec] * 5,
        out_specs=spec,
        out_shape=jax.ShapeDtypeStruct(shp, BF16),
        scratch_shapes=[pltpu.VMEM((seq, 2 * LANES), BF16), pltpu.VMEM((seq, 2 * LANES), BF16)],
        compiler_params=pltpu.CompilerParams(
            dimension_semantics=("arbitrary", "arbitrary", "arbitrary"),
            vmem_limit_bytes=VMEM_LIMIT),
        name="attn_prompt",
    )(q, qx, k, kx, v).reshape(batch * seq, D_ATTN)


def _attn_sample_kernel(q_ref, ck_ref, cv_ref, clf_ref, kn_ref, vn_ref, rn_ref, o_ref,
                        m_ref, l_ref, acc_ref, car_ref, *, tk, nq):
    j = pl.program_id(1)

    def update(h, s, vh):
        m_old = m_ref[h][:, 0:1]
        m_new = jnp.maximum(m_old, jnp.max(s, axis=-1, keepdims=True))
        alpha = jnp.exp(m_old - m_new)
        p = jnp.exp(s - m_new)
        l_new = alpha * l_ref[h][:, 0:1] + jnp.sum(p, axis=-1, keepdims=True)
        acc_ref[h] = alpha * acc_ref[h] + _dot(p.astype(BF16), vh)
        m_ref[h] = jnp.broadcast_to(m_new, (nq, LANES))
        l_ref[h] = jnp.broadcast_to(l_new, (nq, LANES))

    @pl.when(j == 0)
    def _():
        m_ref[...] = jnp.full(m_ref.shape, NEG, F32)
        l_ref[...] = jnp.zeros_like(l_ref)
        acc_ref[...] = jnp.zeros_like(acc_ref)
        car_ref[...] = jnp.zeros_like(car_ref)
        q_id = lax.broadcasted_iota(jnp.int32, (nq, nq), 0)
        k_id = lax.broadcasted_iota(jnp.int32, (nq, nq), 1)
        kn = kn_ref[0]
        vn = vn_ref[0]
        rn = rn_ref[0]
        for h in range(N_HEADS):
            hs = slice(h * HEAD_DIM, (h + 1) * HEAD_DIM)
            s = _dot_nt(q_ref[0, h], kn[:, hs]) + rn[h:h + 1, :]
            update(h, jnp.where(k_id <= q_id, s, NEG), vn[:, hs])

    lf = clf_ref[0]
    lane = lax.broadcasted_iota(jnp.int32, (N_HEADS, tk), 1)
    x = lf
    d = 1
    while d < tk:
        x = x + jnp.where(lane < tk - d, pltpu.roll(x, tk - d, axis=1), 0.0)
        d *= 2
    r = x - lf + car_ref[:, 0:1]
    car_ref[...] = car_ref[...] + x[:, 0:1]

    for h in range(N_HEADS):
        kh = ck_ref[0, pl.ds(h, tk, stride=N_HEADS), :].astype(BF16)
        vh = cv_ref[0, pl.ds(h, tk, stride=N_HEADS), :].astype(BF16)
        update(h, _dot_nt(q_ref[0, h], kh) + r[h:h + 1, :], vh)

    @pl.when(j == pl.num_programs(1) - 1)
    def _():
        for h in range(N_HEADS):
            o = acc_ref[h] / l_ref[h][:, 0:1]
            o_ref[0, :, h * HEAD_DIM:(h + 1) * HEAD_DIM] = o.astype(BF16)


def _attn_sample(qh, cache_k, cache_v, clf_t, k_new, v_new, r_new, *, tk):
    nb, rows, _ = cache_k.shape
    past = rows // N_HEADS
    nq = k_new.shape[1]
    nk = past // tk
    return pl.pallas_call(
        functools.partial(_attn_sample_kernel, tk=tk, nq=nq),
        grid=(nb, nk),
        in_specs=[pl.BlockSpec((1, N_HEADS, nq, HEAD_DIM), lambda b, j: (b, 0, 0, 0)),
                  pl.BlockSpec((1, tk * N_HEADS, HEAD_DIM), lambda b, j: (b, nk - 1 - j, 0)),
                  pl.BlockSpec((1, tk * N_HEADS, HEAD_DIM), lambda b, j: (b, nk - 1 - j, 0)),
                  pl.BlockSpec((1, N_HEADS, tk), lambda b, j: (b, 0, nk - 1 - j)),
                  pl.BlockSpec((1, nq, D_ATTN), lambda b, j: (b, 0, 0)),
                  pl.BlockSpec((1, nq, D_ATTN), lambda b, j: (b, 0, 0)),
                  pl.BlockSpec((1, N_HEADS, nq), lambda b, j: (b, 0, 0))],
        out_specs=pl.BlockSpec((1, nq, D_ATTN), lambda b, j: (b, 0, 0)),
        out_shape=jax.ShapeDtypeStruct((nb, nq, D_ATTN), BF16),
        scratch_shapes=[pltpu.VMEM((N_HEADS, nq, LANES), F32),
                        pltpu.VMEM((N_HEADS, nq, LANES), F32),
                        pltpu.VMEM((N_HEADS, nq, HEAD_DIM), F32),
                        pltpu.VMEM((N_HEADS, LANES), F32)],
        compiler_params=pltpu.CompilerParams(
            dimension_semantics=("arbitrary", "arbitrary"), vmem_limit_bytes=VMEM_LIMIT),
        name="attn_sample",
    )(qh, cache_k, cache_v, clf_t, k_new, v_new, r_new)


def _rms(x, g):
    return x * lax.rsqrt(jnp.mean(x * x, axis=-1, keepdims=True) + RMS_EPS) * g


def _out_ffn_kernel(x_ref, ao_ref, sga_ref, cyg_ref, wao_ref, wout_ref, g2_ref, wg_ref, wu_ref,
                    wd_ref, gf_ref, y_ref):
    ap = _dot(ao_ref[...], wao_ref[...])
    mixed = cyg_ref[...].astype(F32) + sga_ref[...].astype(F32) * ap
    h = x_ref[...] + _dot(mixed.astype(BF16), wout_ref[...])
    z = _rms(h, g2_ref[...]).astype(BF16)
    gate = _dot(z, wg_ref[...])
    act = (gate * jax.nn.sigmoid(gate) * _dot(z, wu_ref[...])).astype(BF16)
    h = h + _dot(act, wd_ref[...])
    y_ref[...] = _rms(h, gf_ref[...])


def _out_ffn(x2d, ao, sga, cyg, w, *, R):
    n_rows = x2d.shape[0]
    row_map = lambda i: (i, 0)
    consts = [w["wao"], w["wout"], w["g2"], w["wg"], w["wu"], w["wd"], w["gf"]]
    in_specs = [pl.BlockSpec((R, D_MODEL), row_map), pl.BlockSpec((R, D_ATTN), row_map),
                pl.BlockSpec((R, D_MODEL), row_map), pl.BlockSpec((R, D_MODEL), row_map)]
    in_specs += [_const_spec(c.shape) for c in consts]
    return pl.pallas_call(
        _out_ffn_kernel,
        grid=(n_rows // R,),
        in_specs=in_specs,
        out_specs=pl.BlockSpec((R, D_MODEL), row_map),
        out_shape=jax.ShapeDtypeStruct((n_rows, D_MODEL), F32),
        compiler_params=pltpu.CompilerParams(
            dimension_semantics=("arbitrary",), vmem_limit_bytes=VMEM_LIMIT),
        name="out_ffn",
    )(x2d, ao, sga, cyg, *consts)


def _prep_weights(norm_mix_g, w_in, b_f, w_dw, b_dw, ln_g, ln_b, w_conv_pw, w_attn_o, w_out,
                  norm_ffn_g, w_gate, w_up, w_down, final_norm_g):
    o_q = 2 * D_CONV
    o_f = o_q + 3 * D_ATTN
    o_gc = o_f + N_HEADS
    o_ga = o_gc + D_MODEL
    row = lambda a: a.reshape(1, -1).astype(F32)
    return {
        "g1": row(norm_mix_g),
        "wglu": w_in[:, :o_q].astype(BF16),
        "wqkv": w_in[:, o_q:o_f].astype(BF16),
        "wf": jnp.pad(w_in[:, o_f:o_gc], ((0, 0), (0, LANES - N_HEADS))).astype(BF16),
        "bf": jnp.pad(row(b_f), ((0, 0), (0, LANES - N_HEADS))),
        "wgc": w_in[:, o_gc:o_ga].astype(BF16),
        "wga": w_in[:, o_ga:].astype(BF16),
        "wdw": jnp.pad(w_dw.astype(F32), ((0, HIST_PAD - CONV_WIDTH), (0, 0))),
        "bdw": row(b_dw), "lng": row(ln_g), "lnb": row(ln_b),
        "wpw": w_conv_pw.astype(BF16),
        "wao": w_attn_o.astype(BF16), "wout": w_out.astype(BF16), "g2": row(norm_ffn_g),
        "wg": w_gate.astype(BF16), "wu": w_up.astype(BF16), "wd": w_down.astype(BF16),
        "gf": row(final_norm_g),
    }


def kernel(x_prompt, x_sample, cache_k, cache_v, cache_logf, state_conv, norm_mix_g, w_in, b_f,
           w_dw, b_dw, ln_g, ln_b, w_conv_pw, w_attn_o, w_out, norm_ffn_g, w_gate, w_up, w_down,
           final_norm_g):
    B, S, _ = x_prompt.shape
    NB, T, _ = x_sample.shape
    P = cache_k.shape[2]
    w = _prep_weights(norm_mix_g[0], w_in[0], b_f[0], w_dw[0], b_dw[0], ln_g[0], ln_b[0],
                      w_conv_pw[0], w_attn_o[0], w_out[0], norm_ffn_g[0], w_gate[0], w_up[0],
                      w_down[0], final_norm_g)

    xp = x_prompt.reshape(B * S, D_MODEL)
    hist0 = jnp.zeros((B, HIST_PAD, D_CONV), F32)
    q, qx, kx, kb, vb, k5, v5, lf, _, sga, cyg, nh = _inproj(xp, hist0, w, nb=1, T=256, carry=True)
    ao = _attn_prompt(q, qx, kb, kx, vb, batch=B, seq=S, tq=512)
    y_prompt = _out_ffn(xp, ao, sga, cyg, w, R=256).reshape(B, S, D_MODEL)
    k_prompt = k5.reshape(1, B, S, N_HEADS, HEAD_DIM)
    v_prompt = v5.reshape(1, B, S, N_HEADS, HEAD_DIM)
    logf_prompt = lf[:, :N_HEADS].reshape(1, B, S, N_HEADS)
    conv_prompt = nh[:, HIST_PAD - HIST:, :].reshape(1, B, HIST, D_CONV)

    xs = x_sample.reshape(NB * T, D_MODEL)
    hist_s = jnp.pad(state_conv[0].astype(F32), ((0, 0), (HIST_PAD - HIST, 0), (0, 0)))
    q, _, _, kb, vb, k5, v5, lf, c, sga, cyg, nh = _inproj(xs, hist_s, w, nb=NB, T=T, carry=False)
    qh = jnp.transpose(q.reshape(NB, T, N_HEADS, HEAD_DIM), (0, 2, 1, 3))
    r_new = -jnp.transpose(c[:, :N_HEADS].reshape(NB, T, N_HEADS), (0, 2, 1))
    clf_t = jnp.transpose(cache_logf[0].astype(F32), (0, 2, 1))
    ao = _attn_sample(qh, cache_k[0].reshape(NB, P * N_HEADS, HEAD_DIM),
                      cache_v[0].reshape(NB, P * N_HEADS, HEAD_DIM), clf_t,
                      kb.reshape(NB, T, D_ATTN), vb.reshape(NB, T, D_ATTN), r_new, tk=1024)
    y_sample = _out_ffn(xs, ao.reshape(NB * T, D_ATTN), sga, cyg, w, R=NB * T)
    y_sample = y_sample.reshape(NB, T, D_MODEL)
    k_sample = k5.reshape(1, NB, T, N_HEADS, HEAD_DIM)
    v_sample = v5.reshape(1, NB, T, N_HEADS, HEAD_DIM)
    logf_sample = lf[:, :N_HEADS].reshape(1, NB, T, N_HEADS)
    conv_sample = nh[:, HIST_PAD - HIST:, :].reshape(1, NB, HIST, D_CONV)

    return (y_prompt, y_sample, k_prompt, v_prompt, logf_prompt, conv_prompt,
            k_sample, v_sample, logf_sample, conv_sample)
```

```python
import functools

import jax
import jax.numpy as jnp
import numpy as np
from jax import lax
from jax.experimental import pallas as pl
from jax.experimental.pallas import tpu as pltpu

D_MODEL = 1024
N_HEADS = 8
HEAD_DIM = 64
D_ATTN = N_HEADS * HEAD_DIM
D_CONV = D_MODEL // 2
CONV_WIDTH = 31
HIST = CONV_WIDTH - 1
D_FF = 2816
RMS_EPS = 1e-6
LN_EPS = 1e-5
NEG = -1e30
SCALE = HEAD_DIM ** -0.5

LANES = 128
HIST_PAD = 32
CONV_ROWS = 64
N_EXTRA = 6
VMEM_LIMIT = 56 * 1024 * 1024

F32 = jnp.float32
BF16 = jnp.bfloat16


def _dot(a, b):
    return jnp.dot(a, b, preferred_element_type=F32)


def _dot_nt(a, b):
    return lax.dot_general(a, b, (((1,), (1,)), ((), ())), preferred_element_type=F32)


def _const_spec(shape):
    return pl.BlockSpec(shape, lambda *_: (0,) * len(shape), pipeline_mode=pl.Buffered(1))


def _split3(c):
    hi = c.astype(BF16)
    r1 = c - hi.astype(F32)
    mid = r1.astype(BF16)
    lo = (r1 - mid.astype(F32)).astype(BF16)
    return hi, mid, lo


def _inproj_kernel(x_ref, hist_ref, g_ref, wglu_ref, wqkv_ref, wf_ref, bf_ref, wgc_ref, wga_ref,
                   wdw_ref, bdw_ref, lng_ref, lnb_ref, wpw_ref, pq_ref, pk_ref, cq_ref, ck_ref,
                   q_ref, qx_ref, kx_ref, kb_ref, vb_ref, k5_ref, v5_ref, lf_ref, c_ref, sga_ref,
                   cyg_ref, nh_ref, cpad_ref, act_ref, ccar_ref, *, nb, T, carry):
    R = nb * T
    RB = min(T, CONV_ROWS)
    t = pl.program_id(1)

    xf = x_ref[...]
    ms = jnp.mean(xf * xf, axis=-1, keepdims=True)
    xn = (xf * lax.rsqrt(ms + RMS_EPS) * g_ref[...]).astype(BF16)

    glu = _dot(xn, wglu_ref[...])
    cu = glu[:, :D_CONV] * jax.nn.sigmoid(glu[:, D_CONV:])
    if carry:
        @pl.when(t == 0)
        def _():
            cpad_ref[:, 0:HIST_PAD, :] = hist_ref[...]
            ccar_ref[...] = jnp.zeros_like(ccar_ref)
    else:
        cpad_ref[:, 0:HIST_PAD, :] = hist_ref[...]
    for b in range(nb):
        cpad_ref[b, HIST_PAD:HIST_PAD + T, :] = cu[b * T:(b + 1) * T]

    off = HIST_PAD - HIST
    win_rows = RB + HIST_PAD
    for b in range(nb):
        for r0 in range(0, T, RB):
            tiles = []
            for lt in range(D_CONV // LANES):
                ls = slice(lt * LANES, (lt + 1) * LANES)
                win = cpad_ref[b, r0:r0 + win_rows, ls]
                acc = jnp.broadcast_to(bdw_ref[:, ls], (RB, LANES))
                for res in range(8):
                    sh = win if res == 0 else pltpu.roll(win, win_rows - res, axis=0)
                    for a in range(HIST_PAD // 8 + 1):
                        w = 8 * a + res - off
                        if 0 <= w < CONV_WIDTH:
                            acc = acc + sh[8 * a:8 * a + RB] * wdw_ref[w:w + 1, ls]
                tiles.append(acc)
            acc = jnp.concatenate(tiles, axis=1)
            mu = jnp.mean(acc, axis=-1, keepdims=True)
            dv = acc - mu
            var = jnp.mean(dv * dv, axis=-1, keepdims=True)
            y = dv * lax.rsqrt(var + LN_EPS) * lng_ref[...] + lnb_ref[...]
            act_ref[b * T + r0:b * T + r0 + RB, :] = (y * jax.nn.sigmoid(y)).astype(BF16)

    nh_ref[...] = cpad_ref[:, T:T + HIST_PAD, :]
    if carry:
        cpad_ref[:, 0:HIST_PAD, :] = cpad_ref[:, T:T + HIST_PAD, :]

    cy = _dot(act_ref[...], wpw_ref[...])
    cyg_ref[...] = (jax.nn.sigmoid(_dot(xn, wgc_ref[...])) * cy).astype(BF16)
    sga_ref[...] = jax.nn.sigmoid(_dot(xn, wga_ref[...])).astype(BF16)

    qkv = _dot(xn, wqkv_ref[...])
    q_ref[...] = (qkv[:, :D_ATTN] * SCALE).astype(BF16)
    kf = qkv[:, D_ATTN:2 * D_ATTN]
    vf = qkv[:, 2 * D_ATTN:]
    kb_ref[...] = kf.astype(BF16)
    vb_ref[...] = vf.astype(BF16)
    for h in range(N_HEADS):
        hs = slice(h * HEAD_DIM, (h + 1) * HEAD_DIM)
        k5_ref[pl.ds(h, R, stride=N_HEADS), :] = kf[:, hs]
        v5_ref[pl.ds(h, R, stride=N_HEADS), :] = vf[:, hs]

    z = _dot(xn, wf_ref[...]) + bf_ref[...]
    lf = -(jnp.maximum(-z, 0.0) + jnp.log1p(jnp.exp(-jnp.abs(z))))
    lane = lax.broadcasted_iota(jnp.int32, (R, LANES), 1)
    lf = jnp.where(lane < N_HEADS, lf, 0.0)
    lf_ref[...] = lf
    row = lax.broadcasted_iota(jnp.int32, (R, LANES), 0) % T
    c = lf
    d = 1
    while d < T:
        c = c + jnp.where(row >= d, pltpu.roll(c, d, axis=0), 0.0)
        d *= 2
    if carry:
        c = c + ccar_ref[0:1, :]
        ccar_ref[...] = jnp.broadcast_to(c[R - 1:R, :], ccar_ref.shape)
    c_ref[...] = c

    hi, mid, lo = _split3(c)
    qx = _dot(hi, pq_ref[0]) + _dot(mid, pq_ref[1]) + _dot(lo, pq_ref[2]) + cq_ref[...]
    kx = _dot(hi, pk_ref[0]) + _dot(mid, pk_ref[1]) + _dot(lo, pk_ref[2]) + ck_ref[...]
    qx_ref[...] = qx.astype(BF16)
    kx_ref[...] = kx.astype(BF16)


def _extra_lane(h):
    pair, odd = divmod(h, 2)
    return pair * LANES + (0 if odd else HEAD_DIM)


def _placement_constants():
    pq = np.zeros((3, LANES, D_ATTN), np.float32)
    pk = np.zeros((3, LANES, D_ATTN), np.float32)
    cq = np.zeros((1, D_ATTN), np.float32)
    ck = np.zeros((1, D_ATTN), np.float32)
    for h in range(N_HEADS):
        p = _extra_lane(h)
        for part in range(3):
            pq[part, h, p + part] = 1.0
            pk[part, h, p + 3 + part] = -1.0
            cq[0, p + 3 + part] = 1.0
            ck[0, p + part] = 1.0
    return pq, pk, cq, ck


def _inproj(x2d, hist, w, *, nb, T, carry):
    n_rows = x2d.shape[0]
    R = nb * T
    n_seg = hist.shape[0]
    n_outer = n_seg // nb
    n_t = n_rows // (R * n_outer)
    grid = (n_outer, n_t)
    row_map = lambda b, t: (b * n_t + t, 0)
    pq, pk, cq, ck = _placement_constants()

    def rows(width, dtype):
        return jax.ShapeDtypeStruct((n_rows, width), dtype), pl.BlockSpec((R, width), row_map)

    def head_rows():
        return (jax.ShapeDtypeStruct((n_rows * N_HEADS, HEAD_DIM), F32),
                pl.BlockSpec((R * N_HEADS, HEAD_DIM), row_map))

    outs = [rows(D_ATTN, BF16), rows(D_ATTN, BF16), rows(D_ATTN, BF16), rows(D_ATTN, BF16),
            rows(D_ATTN, BF16), head_rows(), head_rows(), rows(LANES, F32), rows(LANES, F32),
            rows(D_MODEL, BF16), rows(D_MODEL, BF16)]
    out_shape = [o[0] for o in outs] + [jax.ShapeDtypeStruct((n_seg, HIST_PAD, D_CONV), F32)]
    out_specs = [o[1] for o in outs] + [pl.BlockSpec((nb, HIST_PAD, D_CONV), lambda b, t: (b, 0, 0))]

    consts = [w["g1"], w["wglu"], w["wqkv"], w["wf"], w["bf"], w["wgc"], w["wga"], w["wdw"],
              w["bdw"], w["lng"], w["lnb"], w["wpw"], jnp.asarray(pq, BF16), jnp.asarray(pk, BF16),
              jnp.asarray(cq), jnp.asarray(ck)]
    in_specs = [pl.BlockSpec((R, D_MODEL), row_map),
                pl.BlockSpec((nb, HIST_PAD, D_CONV), lambda b, t: (b, 0, 0))]
    in_specs += [_const_spec(c.shape) for c in consts]

    return pl.pallas_call(
        functools.partial(_inproj_kernel, nb=nb, T=T, carry=carry),
        grid=grid,
        in_specs=in_specs,
        out_specs=out_specs,
        out_shape=out_shape,
        scratch_shapes=[pltpu.VMEM((nb, T + HIST_PAD, D_CONV), F32),
                        pltpu.VMEM((R, D_CONV), BF16),
                        pltpu.VMEM((8, LANES), F32)],
        compiler_params=pltpu.CompilerParams(
            dimension_semantics=("arbitrary", "arbitrary"), vmem_limit_bytes=VMEM_LIMIT),
        name="inproj_conv",
    )(x2d, hist, *consts)


def _attn_prompt_kernel(q_ref, qx_ref, k_ref, kx_ref, v_ref, o_ref, ka_ref, va_ref, *, tq):
    i = pl.program_id(2)
    lane = lax.broadcasted_iota(jnp.int32, (tq, LANES), 1)
    low = lane < HEAD_DIM
    row0 = pl.multiple_of(i * tq, tq)

    q = q_ref[0]
    qx = qx_ref[0]
    k = k_ref[0]
    kx = kx_ref[0]
    v = v_ref[0]
    one_hi = (lane == HEAD_DIM).astype(BF16)
    one_lo = (lane == 0).astype(BF16)
    ka_ref[pl.ds(row0, tq), 0:LANES] = jnp.where(low, k, kx)
    ka_ref[pl.ds(row0, tq), LANES:2 * LANES] = jnp.where(low, kx, k)
    va_ref[pl.ds(row0, tq), 0:LANES] = jnp.where(low, v, one_hi)
    va_ref[pl.ds(row0, tq), LANES:2 * LANES] = jnp.where(low, one_lo, v)
    qa = (jnp.where(low, q, qx), jnp.where(low, qx, q))

    def step(j, carry, masked):
        col0 = pl.multiple_of(j * tq, tq)
        out = []
        for e in range(2):
            m, acc = carry[e]
            kj = ka_ref[pl.ds(col0, tq), e * LANES:(e + 1) * LANES]
            vj = va_ref[pl.ds(col0, tq), e * LANES:(e + 1) * LANES]
            s = _dot_nt(qa[e], kj)
            if masked:
                r_id = lax.broadcasted_iota(jnp.int32, (tq, tq), 0)
                c_id = lax.broadcasted_iota(jnp.int32, (tq, tq), 1)
                s = jnp.where(c_id <= r_id, s, NEG)
            m_new = jnp.maximum(m, jnp.max(s, axis=-1, keepdims=True))
            alpha = jnp.exp(m - m_new)
            p = jnp.exp(s - m_new).astype(BF16)
            out.append((m_new, alpha * acc + _dot(p, vj)))
        return tuple(out)

    init = tuple((jnp.full((tq, 1), NEG, F32), jnp.zeros((tq, LANES), F32)) for _ in range(2))
    carry = lax.fori_loop(0, i, lambda j, c: step(j, c, False), init)
    (_, acc0), (_, acc1) = step(i, carry, True)
    o0 = acc0 / acc0[:, HEAD_DIM:HEAD_DIM + 1]
    o1 = acc1 / acc1[:, 0:1]
    o_ref[0] = jnp.where(low, o0, o1).astype(BF16)


def _attn_prompt(q, qx, k, kx, v, *, batch, seq, tq):
    shp = (batch, seq, D_ATTN)
    q, qx, k, kx, v = (a.reshape(shp) for a in (q, qx, k, kx, v))
    spec = pl.BlockSpec((1, tq, LANES), lambda b, hp, i: (b, i, hp))
    return pl.pallas_call(
        functools.partial(_attn_prompt_kernel, tq=tq),
        grid=(batch, N_HEADS // 2, seq // tq),
        in_specs=[spec] * 5,
        out_specs=spec,
        out_shape=jax.ShapeDtypeStruct(shp, BF16),
        scratch_shapes=[pltpu.VMEM((seq, 2 * LANES), BF16), pltpu.VMEM((seq, 2 * LANES), BF16)],
        compiler_params=pltpu.CompilerParams(
            dimension_semantics=("arbitrary", "arbitrary", "arbitrary"),
            vmem_limit_bytes=VMEM_LIMIT),
        name="attn_prompt",
    )(q, qx, k, kx, v).reshape(batch * seq, D_ATTN)


def _suffix_sum_kernel(x_ref, o_ref):
    x = x_ref[...]
    n = x.shape[1]
    lane = lax.broadcasted_iota(jnp.int32, x.shape, 1)
    s = x
    d = 1
    while d < n:
        s = s + jnp.where(lane < n - d, pltpu.roll(s, n - d, axis=1), 0.0)
        d *= 2
    o_ref[...] = s - x


def _suffix_sum(x):
    return pl.pallas_call(
        _suffix_sum_kernel,
        out_shape=jax.ShapeDtypeStruct(x.shape, F32),
        compiler_params=pltpu.CompilerParams(vmem_limit_bytes=VMEM_LIMIT),
        name="cache_decay",
    )(x)


def _attn_sample_kernel(q_ref, ck_ref, cv_ref, r_ref, kn_ref, vn_ref, rn_ref, o_ref,
                        m_ref, l_ref, acc_ref, *, nq):
    j = pl.program_id(1)

    def update(h, s, pv):
        m_old = m_ref[h][:, 0:1]
        m_new = jnp.maximum(m_old, jnp.max(s, axis=-1, keepdims=True))
        alpha = jnp.exp(m_old - m_new)
        p = jnp.exp(s - m_new)
        l_new = alpha * l_ref[h][:, 0:1] + jnp.sum(p, axis=-1, keepdims=True)
        acc_ref[h] = alpha * acc_ref[h] + pv(p.astype(BF16))
        m_ref[h] = jnp.broadcast_to(m_new, (nq, LANES))
        l_ref[h] = jnp.broadcast_to(l_new, (nq, LANES))

    @pl.when(j == 0)
    def _():
        m_ref[...] = jnp.full(m_ref.shape, NEG, F32)
        l_ref[...] = jnp.zeros_like(l_ref)
        acc_ref[...] = jnp.zeros_like(acc_ref)
        q_id = lax.broadcasted_iota(jnp.int32, (nq, nq), 0)
        k_id = lax.broadcasted_iota(jnp.int32, (nq, nq), 1)
        kn = kn_ref[0]
        vn = vn_ref[0]
        rn = rn_ref[0]
        for h in range(N_HEADS):
            hs = slice(h * HEAD_DIM, (h + 1) * HEAD_DIM)
            s = _dot_nt(q_ref[0, h], kn[:, hs]) + rn[h:h + 1, :]
            update(h, jnp.where(k_id <= q_id, s, NEG), lambda p: _dot(p, vn[:, hs]))

    r = r_ref[0]
    for h in range(N_HEADS):
        kt = ck_ref[0, h].astype(BF16)
        vt = cv_ref[0, h].astype(BF16)
        update(h, _dot(q_ref[0, h], kt) + r[h:h + 1, :], lambda p: _dot_nt(p, vt))

    @pl.when(j == pl.num_programs(1) - 1)
    def _():
        for h in range(N_HEADS):
            o = acc_ref[h] / l_ref[h][:, 0:1]
            o_ref[0, :, h * HEAD_DIM:(h + 1) * HEAD_DIM] = o.astype(BF16)


def _attn_sample(qh, cache_kt, cache_vt, r_cache, k_new, v_new, r_new, *, tk):
    nb, _, _, past = cache_kt.shape
    nq = k_new.shape[1]
    cache_spec = pl.BlockSpec((1, N_HEADS, HEAD_DIM, tk), lambda b, j: (b, 0, 0, j))
    return pl.pallas_call(
        functools.partial(_attn_sample_kernel, nq=nq),
        grid=(nb, past // tk),
        in_specs=[pl.BlockSpec((1, N_HEADS, nq, HEAD_DIM), lambda b, j: (b, 0, 0, 0)),
                  cache_spec, cache_spec,
                  pl.BlockSpec((1, N_HEADS, tk), lambda b, j: (b, 0, j)),
                  pl.BlockSpec((1, nq, D_ATTN), lambda b, j: (b, 0, 0)),
                  pl.BlockSpec((1, nq, D_ATTN), lambda b, j: (b, 0, 0)),
                  pl.BlockSpec((1, N_HEADS, nq), lambda b, j: (b, 0, 0))],
        out_specs=pl.BlockSpec((1, nq, D_ATTN), lambda b, j: (b, 0, 0)),
        out_shape=jax.ShapeDtypeStruct((nb, nq, D_ATTN), BF16),
        scratch_shapes=[pltpu.VMEM((N_HEADS, nq, LANES), F32),
                        pltpu.VMEM((N_HEADS, nq, LANES), F32),
                        pltpu.VMEM((N_HEADS, nq, HEAD_DIM), F32)],
        compiler_params=pltpu.CompilerParams(
            dimension_semantics=("arbitrary", "arbitrary"), vmem_limit_bytes=VMEM_LIMIT),
        name="attn_sample",
    )(qh, cache_kt, cache_vt, r_cache, k_new, v_new, r_new)


def _rms(x, g):
    return x * lax.rsqrt(jnp.mean(x * x, axis=-1, keepdims=True) + RMS_EPS) * g


def _out_ffn_kernel(x_ref, ao_ref, sga_ref, cyg_ref, wao_ref, wout_ref, g2_ref, wg_ref, wu_ref,
                    wd_ref, gf_ref, y_ref):
    ap = _dot(ao_ref[...], wao_ref[...])
    mixed = cyg_ref[...].astype(F32) + sga_ref[...].astype(F32) * ap
    h = x_ref[...] + _dot(mixed.astype(BF16), wout_ref[...])
    z = _rms(h, g2_ref[...]).astype(BF16)
    gate = _dot(z, wg_ref[...])
    act = (gate * jax.nn.sigmoid(gate) * _dot(z, wu_ref[...])).astype(BF16)
    h = h + _dot(act, wd_ref[...])
    y_ref[...] = _rms(h, gf_ref[...])


def _out_ffn(x2d, ao, sga, cyg, w, *, R):
    n_rows = x2d.shape[0]
    row_map = lambda i: (i, 0)
    consts = [w["wao"], w["wout"], w["g2"], w["wg"], w["wu"], w["wd"], w["gf"]]
    in_specs = [pl.BlockSpec((R, D_MODEL), row_map), pl.BlockSpec((R, D_ATTN), row_map),
                pl.BlockSpec((R, D_MODEL), row_map), pl.BlockSpec((R, D_MODEL), row_map)]
    in_specs += [_const_spec(c.shape) for c in consts]
    return pl.pallas_call(
        _out_ffn_kernel,
        grid=(n_rows // R,),
        in_specs=in_specs,
        out_specs=pl.BlockSpec((R, D_MODEL), row_map),
        out_shape=jax.ShapeDtypeStruct((n_rows, D_MODEL), F32),
        compiler_params=pltpu.CompilerParams(
            dimension_semantics=("arbitrary",), vmem_limit_bytes=VMEM_LIMIT),
        name="out_ffn",
    )(x2d, ao, sga, cyg, *consts)


def _prep_weights(norm_mix_g, w_in, b_f, w_dw, b_dw, ln_g, ln_b, w_conv_pw, w_attn_o, w_out,
                  norm_ffn_g, w_gate, w_up, w_down, final_norm_g):
    o_q = 2 * D_CONV
    o_f = o_q + 3 * D_ATTN
    o_gc = o_f + N_HEADS
    o_ga = o_gc + D_MODEL
    row = lambda a: a.reshape(1, -1).astype(F32)
    return {
        "g1": row(norm_mix_g),
        "wglu": w_in[:, :o_q].astype(BF16),
        "wqkv": w_in[:, o_q:o_f].astype(BF16),
        "wf": jnp.pad(w_in[:, o_f:o_gc], ((0, 0), (0, LANES - N_HEADS))).astype(BF16),
        "bf": jnp.pad(row(b_f), ((0, 0), (0, LANES - N_HEADS))),
        "wgc": w_in[:, o_gc:o_ga].astype(BF16),
        "wga": w_in[:, o_ga:].astype(BF16),
        "wdw": jnp.pad(w_dw.astype(F32), ((0, HIST_PAD - CONV_WIDTH), (0, 0))),
        "bdw": row(b_dw), "lng": row(ln_g), "lnb": row(ln_b),
        "wpw": w_conv_pw.astype(BF16),
        "wao": w_attn_o.astype(BF16), "wout": w_out.astype(BF16), "g2": row(norm_ffn_g),
        "wg": w_gate.astype(BF16), "wu": w_up.astype(BF16), "wd": w_down.astype(BF16),
        "gf": row(final_norm_g),
    }


def kernel(x_prompt, x_sample, cache_k, cache_v, cache_logf, state_conv, norm_mix_g, w_in, b_f,
           w_dw, b_dw, ln_g, ln_b, w_conv_pw, w_attn_o, w_out, norm_ffn_g, w_gate, w_up, w_down,
           final_norm_g):
    B, S, _ = x_prompt.shape
    NB, T, _ = x_sample.shape
    P = cache_k.shape[2]
    w = _prep_weights(norm_mix_g[0], w_in[0], b_f[0], w_dw[0], b_dw[0], ln_g[0], ln_b[0],
                      w_conv_pw[0], w_attn_o[0], w_out[0], norm_ffn_g[0], w_gate[0], w_up[0],
                      w_down[0], final_norm_g)

    xp = x_prompt.reshape(B * S, D_MODEL)
    hist0 = jnp.zeros((B, HIST_PAD, D_CONV), F32)
    q, qx, kx, kb, vb, k5, v5, lf, _, sga, cyg, nh = _inproj(xp, hist0, w, nb=1, T=256, carry=True)
    ao = _attn_prompt(q, qx, kb, kx, vb, batch=B, seq=S, tq=512)
    y_prompt = _out_ffn(xp, ao, sga, cyg, w, R=256).reshape(B, S, D_MODEL)
    k_prompt = k5.reshape(1, B, S, N_HEADS, HEAD_DIM)
    v_prompt = v5.reshape(1, B, S, N_HEADS, HEAD_DIM)
    logf_prompt = lf[:, :N_HEADS].reshape(1, B, S, N_HEADS)
    conv_prompt = nh[:, HIST_PAD - HIST:, :].reshape(1, B, HIST, D_CONV)

    xs = x_sample.reshape(NB * T, D_MODEL)
    hist_s = jnp.pad(state_conv[0].astype(F32), ((0, 0), (HIST_PAD - HIST, 0), (0, 0)))
    q, _, _, kb, vb, k5, v5, lf, c, sga, cyg, nh = _inproj(xs, hist_s, w, nb=NB, T=T, carry=False)
    qh = jnp.transpose(q.reshape(NB, T, N_HEADS, HEAD_DIM), (0, 2, 1, 3))
    r_new = -jnp.transpose(c[:, :N_HEADS].reshape(NB, T, N_HEADS), (0, 2, 1))
    clf_t = jnp.transpose(cache_logf[0].astype(F32), (0, 2, 1)).reshape(NB * N_HEADS, P)
    r_cache = _suffix_sum(clf_t).reshape(NB, N_HEADS, P)
    ao = _attn_sample(qh, jnp.transpose(cache_k[0], (0, 2, 3, 1)),
                      jnp.transpose(cache_v[0], (0, 2, 3, 1)), r_cache,
                      kb.reshape(NB, T, D_ATTN), vb.reshape(NB, T, D_ATTN), r_new, tk=2048)
    y_sample = _out_ffn(xs, ao.reshape(NB * T, D_ATTN), sga, cyg, w, R=NB * T)
    y_sample = y_sample.reshape(NB, T, D_MODEL)
    k_sample = k5.reshape(1, NB, T, N_HEADS, HEAD_DIM)
    v_sample = v5.reshape(1, NB, T, N_HEADS, HEAD_DIM)
    logf_sample = lf[:, :N_HEADS].reshape(1, NB, T, N_HEADS)
    conv_sample = nh[:, HIST_PAD - HIST:, :].reshape(1, NB, HIST, D_CONV)

    return (y_prompt, y_sample, k_prompt, v_prompt, logf_prompt, conv_prompt,
            k_sample, v_sample, logf_sample, conv_sample)
```

```python
import functools

import jax
import jax.numpy as jnp
import numpy as np
from jax import lax
from jax.experimental import pallas as pl
from jax.experimental.pallas import tpu as pltpu

D_MODEL = 1024
N_HEADS = 8
HEAD_DIM = 64
D_ATTN = N_HEADS * HEAD_DIM
D_CONV = D_MODEL // 2
CONV_WIDTH = 31
HIST = CONV_WIDTH - 1
D_FF = 2816
RMS_EPS = 1e-6
LN_EPS = 1e-5
NEG = -1e30
SCALE = HEAD_DIM ** -0.5

LANES = 128
HIST_PAD = 32
CONV_ROWS = 64
ROW_BLOCK = 128
VMEM_LIMIT = 56 * 1024 * 1024

F32 = jnp.float32
BF16 = jnp.bfloat16


def _dot(a, b):
    return jnp.dot(a, b, preferred_element_type=F32)


def _dot_nt(a, b):
    return lax.dot_general(a, b, (((1,), (1,)), ((), ())), preferred_element_type=F32)


def _const_spec(shape):
    return pl.BlockSpec(shape, lambda *_: (0,) * len(shape), pipeline_mode=pl.Buffered(1))


def _split3(c):
    hi = c.astype(BF16)
    r1 = c - hi.astype(F32)
    mid = r1.astype(BF16)
    lo = (r1 - mid.astype(F32)).astype(BF16)
    return hi, mid, lo


def _inproj_kernel(x_ref, hist_ref, g_ref, wglu_ref, wqkv_ref, wf_ref, bf_ref, wgc_ref, wga_ref,
                   wdw_ref, bdw_ref, lng_ref, lnb_ref, wpw_ref, pq_ref, pk_ref, cq_ref, ck_ref,
                   q_ref, qx_ref, kx_ref, kb_ref, vb_ref, k5_ref, v5_ref, lf_ref, c_ref, sga_ref,
                   cyg_ref, nh_ref, cpad_ref, act_ref, ccar_ref, *, nb, T, carry):
    R = nb * T
    RB = min(T, CONV_ROWS)
    t = pl.program_id(1)

    xf = x_ref[...]
    ms = jnp.mean(xf * xf, axis=-1, keepdims=True)
    xn = (xf * lax.rsqrt(ms + RMS_EPS) * g_ref[...]).astype(BF16)

    glu = _dot(xn, wglu_ref[...])
    cu = glu[:, :D_CONV] * jax.nn.sigmoid(glu[:, D_CONV:])
    if carry:
        @pl.when(t == 0)
        def _():
            cpad_ref[:, 0:HIST_PAD, :] = hist_ref[...]
            ccar_ref[...] = jnp.zeros_like(ccar_ref)
    else:
        cpad_ref[:, 0:HIST_PAD, :] = hist_ref[...]
    for b in range(nb):
        cpad_ref[b, HIST_PAD:HIST_PAD + T, :] = cu[b * T:(b + 1) * T]

    off = HIST_PAD - HIST
    win_rows = RB + HIST_PAD
    for b in range(nb):
        for r0 in range(0, T, RB):
            tiles = []
            for lt in range(D_CONV // LANES):
                ls = slice(lt * LANES, (lt + 1) * LANES)
                win = cpad_ref[b, r0:r0 + win_rows, ls]
                acc = jnp.broadcast_to(bdw_ref[:, ls], (RB, LANES))
                for res in range(8):
                    sh = win if res == 0 else pltpu.roll(win, win_rows - res, axis=0)
                    for a in range(HIST_PAD // 8 + 1):
                        w = 8 * a + res - off
                        if 0 <= w < CONV_WIDTH:
                            acc = acc + sh[8 * a:8 * a + RB] * wdw_ref[w:w + 1, ls]
                tiles.append(acc)
            acc = jnp.concatenate(tiles, axis=1)
            mu = jnp.mean(acc, axis=-1, keepdims=True)
            dv = acc - mu
            var = jnp.mean(dv * dv, axis=-1, keepdims=True)
            y = dv * lax.rsqrt(var + LN_EPS) * lng_ref[...] + lnb_ref[...]
            act_ref[b * T + r0:b * T + r0 + RB, :] = (y * jax.nn.sigmoid(y)).astype(BF16)

    nh_ref[...] = cpad_ref[:, T:T + HIST_PAD, :]
    if carry:
        cpad_ref[:, 0:HIST_PAD, :] = cpad_ref[:, T:T + HIST_PAD, :]

    cy = _dot(act_ref[...], wpw_ref[...])
    cyg_ref[...] = (jax.nn.sigmoid(_dot(xn, wgc_ref[...])) * cy).astype(BF16)
    sga_ref[...] = jax.nn.sigmoid(_dot(xn, wga_ref[...])).astype(BF16)

    qkv = _dot(xn, wqkv_ref[...])
    q_ref[...] = (qkv[:, :D_ATTN] * SCALE).astype(BF16)
    kf = qkv[:, D_ATTN:2 * D_ATTN]
    vf = qkv[:, 2 * D_ATTN:]
    kb_ref[...] = kf.astype(BF16)
    vb_ref[...] = vf.astype(BF16)
    for h in range(N_HEADS):
        hs = slice(h * HEAD_DIM, (h + 1) * HEAD_DIM)
        k5_ref[pl.ds(h, R, stride=N_HEADS), :] = kf[:, hs]
        v5_ref[pl.ds(h, R, stride=N_HEADS), :] = vf[:, hs]

    z = _dot(xn, wf_ref[...]) + bf_ref[...]
    lf = -(jnp.maximum(-z, 0.0) + jnp.log1p(jnp.exp(-jnp.abs(z))))
    lane = lax.broadcasted_iota(jnp.int32, (R, LANES), 1)
    lf = jnp.where(lane < N_HEADS, lf, 0.0)
    lf_ref[...] = lf
    row = lax.broadcasted_iota(jnp.int32, (R, LANES), 0) % T
    c = lf
    d = 1
    while d < T:
        c = c + jnp.where(row >= d, pltpu.roll(c, d, axis=0), 0.0)
        d *= 2
    if carry:
        c = c + ccar_ref[0:1, :]
        ccar_ref[...] = jnp.broadcast_to(c[R - 1:R, :], ccar_ref.shape)
    c_ref[...] = c

    hi, mid, lo = _split3(c)
    qx = _dot(hi, pq_ref[0]) + _dot(mid, pq_ref[1]) + _dot(lo, pq_ref[2]) + cq_ref[...]
    kx = _dot(hi, pk_ref[0]) + _dot(mid, pk_ref[1]) + _dot(lo, pk_ref[2]) + ck_ref[...]
    qx_ref[...] = qx.astype(BF16)
    kx_ref[...] = kx.astype(BF16)


def _extra_lane(h):
    pair, odd = divmod(h, 2)
    return pair * LANES + (0 if odd else HEAD_DIM)


def _placement_constants():
    pq = np.zeros((3, LANES, D_ATTN), np.float32)
    pk = np.zeros((3, LANES, D_ATTN), np.float32)
    cq = np.zeros((1, D_ATTN), np.float32)
    ck = np.zeros((1, D_ATTN), np.float32)
    for h in range(N_HEADS):
        p = _extra_lane(h)
        for part in range(3):
            pq[part, h, p + part] = 1.0
            pk[part, h, p + 3 + part] = -1.0
            cq[0, p + 3 + part] = 1.0
            ck[0, p + part] = 1.0
    return pq, pk, cq, ck


def _inproj(x2d, hist, w, *, nb, T, carry):
    n_rows = x2d.shape[0]
    R = nb * T
    n_seg = hist.shape[0]
    n_outer = n_seg // nb
    n_t = n_rows // (R * n_outer)
    grid = (n_outer, n_t)
    row_map = lambda b, t: (b * n_t + t, 0)
    pq, pk, cq, ck = _placement_constants()

    def rows(width, dtype):
        return jax.ShapeDtypeStruct((n_rows, width), dtype), pl.BlockSpec((R, width), row_map)

    def head_rows():
        return (jax.ShapeDtypeStruct((n_rows * N_HEADS, HEAD_DIM), F32),
                pl.BlockSpec((R * N_HEADS, HEAD_DIM), row_map))

    outs = [rows(D_ATTN, BF16), rows(D_ATTN, BF16), rows(D_ATTN, BF16), rows(D_ATTN, BF16),
            rows(D_ATTN, BF16), head_rows(), head_rows(), rows(LANES, F32), rows(LANES, F32),
            rows(D_MODEL, BF16), rows(D_MODEL, BF16)]
    out_shape = [o[0] for o in outs] + [jax.ShapeDtypeStruct((n_seg, HIST_PAD, D_CONV), F32)]
    out_specs = [o[1] for o in outs] + [pl.BlockSpec((nb, HIST_PAD, D_CONV), lambda b, t: (b, 0, 0))]

    consts = [w["g1"], w["wglu"], w["wqkv"], w["wf"], w["bf"], w["wgc"], w["wga"], w["wdw"],
              w["bdw"], w["lng"], w["lnb"], w["wpw"], jnp.asarray(pq, BF16), jnp.asarray(pk, BF16),
              jnp.asarray(cq), jnp.asarray(ck)]
    in_specs = [pl.BlockSpec((R, D_MODEL), row_map),
                pl.BlockSpec((nb, HIST_PAD, D_CONV), lambda b, t: (b, 0, 0))]
    in_specs += [_const_spec(c.shape) for c in consts]

    return pl.pallas_call(
        functools.partial(_inproj_kernel, nb=nb, T=T, carry=carry),
        grid=grid,
        in_specs=in_specs,
        out_specs=out_specs,
        out_shape=out_shape,
        scratch_shapes=[pltpu.VMEM((nb, T + HIST_PAD, D_CONV), F32),
                        pltpu.VMEM((R, D_CONV), BF16),
                        pltpu.VMEM((8, LANES), F32)],
        compiler_params=pltpu.CompilerParams(
            dimension_semantics=("arbitrary", "arbitrary"), vmem_limit_bytes=VMEM_LIMIT),
        name="inproj_conv",
    )(x2d, hist, *consts)


def _attn_prompt_kernel(q_ref, qx_ref, k_ref, kx_ref, v_ref, o_ref, ka_ref, va_ref, *, tq, tk, hps):
    i = pl.program_id(2)
    nh = 2 * hps
    lane = lax.broadcasted_iota(jnp.int32, (tq, LANES), 1)
    low = lane < HEAD_DIM
    row0 = pl.multiple_of(i * tq, tq)
    one_hi = (lane == HEAD_DIM).astype(BF16)
    one_lo = (lane == 0).astype(BF16)

    qa = []
    for hp in range(hps):
        ps = slice(hp * LANES, (hp + 1) * LANES)
        ev = slice(2 * hp * LANES, (2 * hp + 1) * LANES)
        od = slice((2 * hp + 1) * LANES, (2 * hp + 2) * LANES)
        q, qx, k, kx, v = (r[0, :, ps] for r in (q_ref, qx_ref, k_ref, kx_ref, v_ref))
        ka_ref[pl.ds(row0, tq), ev] = jnp.where(low, k, kx)
        ka_ref[pl.ds(row0, tq), od] = jnp.where(low, kx, k)
        va_ref[pl.ds(row0, tq), ev] = jnp.where(low, v, one_hi)
        va_ref[pl.ds(row0, tq), od] = jnp.where(low, one_lo, v)
        qa += [jnp.where(low, q, qx), jnp.where(low, qx, q)]

    def step(col0, carry, diag=None):
        r_lo = 0 if diag is None else diag * tk
        out = []
        for e in range(nh):
            m, acc = carry[e]
            kj = ka_ref[pl.ds(col0, tk), e * LANES:(e + 1) * LANES]
            vj = va_ref[pl.ds(col0, tk), e * LANES:(e + 1) * LANES]
            s = _dot_nt(qa[e][r_lo:], kj)
            if diag is not None:
                r_id = lax.broadcasted_iota(jnp.int32, (tq - r_lo, tk), 0)
                c_id = lax.broadcasted_iota(jnp.int32, (tq - r_lo, tk), 1)
                s = jnp.where(c_id <= r_id, s, NEG)
            m_new = jnp.maximum(m[r_lo:], jnp.max(s, axis=-1, keepdims=True))
            alpha = jnp.exp(m[r_lo:] - m_new)
            p = jnp.exp(s - m_new).astype(BF16)
            acc_new = alpha * acc[r_lo:] + _dot(p, vj)
            if r_lo:
                m_new = jnp.concatenate([m[:r_lo], m_new], axis=0)
                acc_new = jnp.concatenate([acc[:r_lo], acc_new], axis=0)
            out.append((m_new, acc_new))
        return tuple(out)

    init = tuple((jnp.full((tq, 1), NEG, F32), jnp.zeros((tq, LANES), F32)) for _ in range(nh))
    carry = lax.fori_loop(0, i * (tq // tk),
                          lambda j, c: step(pl.multiple_of(j * tk, tk), c), init)
    for d in range(tq // tk):
        carry = step(pl.multiple_of(row0 + d * tk, tk), carry, diag=d)
    for hp in range(hps):
        acc0, acc1 = carry[2 * hp][1], carry[2 * hp + 1][1]
        o0 = acc0 / acc0[:, HEAD_DIM:HEAD_DIM + 1]
        o1 = acc1 / acc1[:, 0:1]
        o_ref[0, :, hp * LANES:(hp + 1) * LANES] = jnp.where(low, o0, o1).astype(BF16)


def _attn_prompt(q, qx, k, kx, v, *, batch, seq, tq, tk, hps):
    shp = (batch, seq, D_ATTN)
    q, qx, k, kx, v = (a.reshape(shp) for a in (q, qx, k, kx, v))
    spec = pl.BlockSpec((1, tq, hps * LANES), lambda b, g, i: (b, i, g))
    return pl.pallas_call(
        functools.partial(_attn_prompt_kernel, tq=tq, tk=tk, hps=hps),
        grid=(batch, N_HEADS // (2 * hps), seq // tq),
        in_specs=[spec] * 5,
        out_specs=spec,
        out_shape=jax.ShapeDtypeStruct(shp, BF16),
        scratch_shapes=[pltpu.VMEM((seq, 2 * hps * LANES), BF16),
                        pltpu.VMEM((seq, 2 * hps * LANES), BF16)],
        compiler_params=pltpu.CompilerParams(
            dimension_semantics=("arbitrary", "arbitrary", "arbitrary"),
            vmem_limit_bytes=VMEM_LIMIT),
        name="attn_prompt",
    )(q, qx, k, kx, v).reshape(batch * seq, D_ATTN)


def _suffix_sum_kernel(x_ref, o_ref):
    x = x_ref[...]
    n = x.shape[1]
    lane = lax.broadcasted_iota(jnp.int32, x.shape, 1)
    s = x
    d = 1
    while d < n:
        s = s + jnp.where(lane < n - d, pltpu.roll(s, n - d, axis=1), 0.0)
        d *= 2
    o_ref[...] = s - x


def _suffix_sum(x):
    return pl.pallas_call(
        _suffix_sum_kernel,
        out_shape=jax.ShapeDtypeStruct(x.shape, F32),
        compiler_params=pltpu.CompilerParams(vmem_limit_bytes=VMEM_LIMIT),
        name="cache_decay",
    )(x)


def _attn_sample_kernel(q_ref, ck_ref, cv_ref, r_ref, kn_ref, vn_ref, rn_ref, o_ref,
                        m_ref, l_ref, acc_ref, *, nq):
    j = pl.program_id(1)

    def update(h, s, pv):
        m_old = m_ref[h][:, 0:1]
        m_new = jnp.maximum(m_old, jnp.max(s, axis=-1, keepdims=True))
        alpha = jnp.exp(m_old - m_new)
        p = jnp.exp(s - m_new)
        l_new = alpha * l_ref[h][:, 0:1] + jnp.sum(p, axis=-1, keepdims=True)
        acc_ref[h] = alpha * acc_ref[h] + pv(p.astype(BF16))
        m_ref[h] = jnp.broadcast_to(m_new, (nq, LANES))
        l_ref[h] = jnp.broadcast_to(l_new, (nq, LANES))

    @pl.when(j == 0)
    def _():
        m_ref[...] = jnp.full(m_ref.shape, NEG, F32)
        l_ref[...] = jnp.zeros_like(l_ref)
        acc_ref[...] = jnp.zeros_like(acc_ref)
        q_id = lax.broadcasted_iota(jnp.int32, (nq, nq), 0)
        k_id = lax.broadcasted_iota(jnp.int32, (nq, nq), 1)
        kn = kn_ref[0]
        vn = vn_ref[0]
        rn = rn_ref[0]
        for h in range(N_HEADS):
            hs = slice(h * HEAD_DIM, (h + 1) * HEAD_DIM)
            s = _dot_nt(q_ref[0, h], kn[:, hs]) + rn[h:h + 1, :]
            update(h, jnp.where(k_id <= q_id, s, NEG), lambda p: _dot(p, vn[:, hs]))

    r = r_ref[0]
    for h in range(N_HEADS):
        kt = ck_ref[0, h].astype(BF16)
        vt = cv_ref[0, h].astype(BF16)
        update(h, _dot(q_ref[0, h], kt) + r[h:h + 1, :], lambda p: _dot_nt(p, vt))

    @pl.when(j == pl.num_programs(1) - 1)
    def _():
        for h in range(N_HEADS):
            o = acc_ref[h] / l_ref[h][:, 0:1]
            o_ref[0, :, h * HEAD_DIM:(h + 1) * HEAD_DIM] = o.astype(BF16)


def _attn_sample(qh, cache_kt, cache_vt, r_cache, k_new, v_new, r_new, *, tk):
    nb, _, _, past = cache_kt.shape
    nq = k_new.shape[1]
    cache_spec = pl.BlockSpec((1, N_HEADS, HEAD_DIM, tk), lambda b, j: (b, 0, 0, j))
    return pl.pallas_call(
        functools.partial(_attn_sample_kernel, nq=nq),
        grid=(nb, past // tk),
        in_specs=[pl.BlockSpec((1, N_HEADS, nq, HEAD_DIM), lambda b, j: (b, 0, 0, 0)),
                  cache_spec, cache_spec,
                  pl.BlockSpec((1, N_HEADS, tk), lambda b, j: (b, 0, j)),
                  pl.BlockSpec((1, nq, D_ATTN), lambda b, j: (b, 0, 0)),
                  pl.BlockSpec((1, nq, D_ATTN), lambda b, j: (b, 0, 0)),
                  pl.BlockSpec((1, N_HEADS, nq), lambda b, j: (b, 0, 0))],
        out_specs=pl.BlockSpec((1, nq, D_ATTN), lambda b, j: (b, 0, 0)),
        out_shape=jax.ShapeDtypeStruct((nb, nq, D_ATTN), BF16),
        scratch_shapes=[pltpu.VMEM((N_HEADS, nq, LANES), F32),
                        pltpu.VMEM((N_HEADS, nq, LANES), F32),
                        pltpu.VMEM((N_HEADS, nq, HEAD_DIM), F32)],
        compiler_params=pltpu.CompilerParams(
            dimension_semantics=("arbitrary", "arbitrary"), vmem_limit_bytes=VMEM_LIMIT),
        name="attn_sample",
    )(qh, cache_kt, cache_vt, r_cache, k_new, v_new, r_new)


def _rms(x, g):
    return x * lax.rsqrt(jnp.mean(x * x, axis=-1, keepdims=True) + RMS_EPS) * g


def _out_ffn_kernel(x_ref, ao_ref, sga_ref, cyg_ref, wao_ref, wout_ref, g2_ref, wg_ref, wu_ref,
                    wd_ref, gf_ref, y_ref):
    ap = _dot(ao_ref[...], wao_ref[...])
    mixed = cyg_ref[...].astype(F32) + sga_ref[...].astype(F32) * ap
    h = x_ref[...] + _dot(mixed.astype(BF16), wout_ref[...])
    z = _rms(h, g2_ref[...]).astype(BF16)
    gate = _dot(z, wg_ref[...])
    act = (gate * jax.nn.sigmoid(gate) * _dot(z, wu_ref[...])).astype(BF16)
    h = h + _dot(act, wd_ref[...])
    y_ref[...] = _rms(h, gf_ref[...])


def _out_ffn(x2d, ao, sga, cyg, w, *, R):
    n_rows = x2d.shape[0]
    row_map = lambda i: (i, 0)
    consts = [w["wao"], w["wout"], w["g2"], w["wg"], w["wu"], w["wd"], w["gf"]]
    in_specs = [pl.BlockSpec((R, D_MODEL), row_map), pl.BlockSpec((R, D_ATTN), row_map),
                pl.BlockSpec((R, D_MODEL), row_map), pl.BlockSpec((R, D_MODEL), row_map)]
    in_specs += [_const_spec(c.shape) for c in consts]
    return pl.pallas_call(
        _out_ffn_kernel,
        grid=(n_rows // R,),
        in_specs=in_specs,
        out_specs=pl.BlockSpec((R, D_MODEL), row_map),
        out_shape=jax.ShapeDtypeStruct((n_rows, D_MODEL), F32),
        compiler_params=pltpu.CompilerParams(
            dimension_semantics=("arbitrary",), vmem_limit_bytes=VMEM_LIMIT),
        name="out_ffn",
    )(x2d, ao, sga, cyg, *consts)


def _prep_weights(norm_mix_g, w_in, b_f, w_dw, b_dw, ln_g, ln_b, w_conv_pw, w_attn_o, w_out,
                  norm_ffn_g, w_gate, w_up, w_down, final_norm_g):
    o_q = 2 * D_CONV
    o_f = o_q + 3 * D_ATTN
    o_gc = o_f + N_HEADS
    o_ga = o_gc + D_MODEL
    row = lambda a: a.reshape(1, -1).astype(F32)
    return {
        "g1": row(norm_mix_g),
        "wglu": w_in[:, :o_q].astype(BF16),
        "wqkv": w_in[:, o_q:o_f].astype(BF16),
        "wf": jnp.pad(w_in[:, o_f:o_gc], ((0, 0), (0, LANES - N_HEADS))).astype(BF16),
        "bf": jnp.pad(row(b_f), ((0, 0), (0, LANES - N_HEADS))),
        "wgc": w_in[:, o_gc:o_ga].astype(BF16),
        "wga": w_in[:, o_ga:].astype(BF16),
        "wdw": jnp.pad(w_dw.astype(F32), ((0, HIST_PAD - CONV_WIDTH), (0, 0))),
        "bdw": row(b_dw), "lng": row(ln_g), "lnb": row(ln_b),
        "wpw": w_conv_pw.astype(BF16),
        "wao": w_attn_o.astype(BF16), "wout": w_out.astype(BF16), "g2": row(norm_ffn_g),
        "wg": w_gate.astype(BF16), "wu": w_up.astype(BF16), "wd": w_down.astype(BF16),
        "gf": row(final_norm_g),
    }


def kernel(x_prompt, x_sample, cache_k, cache_v, cache_logf, state_conv, norm_mix_g, w_in, b_f,
           w_dw, b_dw, ln_g, ln_b, w_conv_pw, w_attn_o, w_out, norm_ffn_g, w_gate, w_up, w_down,
           final_norm_g):
    B, S, _ = x_prompt.shape
    NB, T, _ = x_sample.shape
    P = cache_k.shape[2]
    w = _prep_weights(norm_mix_g[0], w_in[0], b_f[0], w_dw[0], b_dw[0], ln_g[0], ln_b[0],
                      w_conv_pw[0], w_attn_o[0], w_out[0], norm_ffn_g[0], w_gate[0], w_up[0],
                      w_down[0], final_norm_g)

    xp = x_prompt.reshape(B * S, D_MODEL)
    hist0 = jnp.zeros((B, HIST_PAD, D_CONV), F32)
    q, qx, kx, kb, vb, k5, v5, lf, _, sga, cyg, nh = _inproj(xp, hist0, w, nb=1, T=256, carry=True)
    ao = _attn_prompt(q, qx, kb, kx, vb, batch=B, seq=S, tq=1024, tk=1024, hps=1)
    y_prompt = _out_ffn(xp, ao, sga, cyg, w, R=256).reshape(B, S, D_MODEL)
    k_prompt = k5.reshape(1, B, S, N_HEADS, HEAD_DIM)
    v_prompt = v5.reshape(1, B, S, N_HEADS, HEAD_DIM)
    logf_prompt = lf[:, :N_HEADS].reshape(1, B, S, N_HEADS)
    conv_prompt = nh[:, HIST_PAD - HIST:, :].reshape(1, B, HIST, D_CONV)

    xs = x_sample.reshape(NB * T, D_MODEL)
    hist_s = jnp.pad(state_conv[0].astype(F32), ((0, 0), (HIST_PAD - HIST, 0), (0, 0)))
    q, _, _, kb, vb, k5, v5, lf, c, sga, cyg, nh = _inproj(xs, hist_s, w, nb=NB, T=T, carry=False)
    qh = jnp.transpose(q.reshape(NB, T, N_HEADS, HEAD_DIM), (0, 2, 1, 3))
    r_new = -jnp.transpose(c[:, :N_HEADS].reshape(NB, T, N_HEADS), (0, 2, 1))
    clf_t = jnp.transpose(cache_logf[0].astype(F32), (0, 2, 1)).reshape(NB * N_HEADS, P)
    r_cache = _suffix_sum(clf_t).reshape(NB, N_HEADS, P)
    ao = _attn_sample(qh, jnp.transpose(cache_k[0], (0, 2, 3, 1)),
                      jnp.transpose(cache_v[0], (0, 2, 3, 1)), r_cache,
                      kb.reshape(NB, T, D_ATTN), vb.reshape(NB, T, D_ATTN), r_new, tk=2048)
    y_sample = _out_ffn(xs, ao.reshape(NB * T, D_ATTN), sga, cyg, w, R=NB * T)
    y_sample = y_sample.reshape(NB, T, D_MODEL)
    k_sample = k5.reshape(1, NB, T, N_HEADS, HEAD_DIM)
    v_sample = v5.reshape(1, NB, T, N_HEADS, HEAD_DIM)
    logf_sample = lf[:, :N_HEADS].reshape(1, NB, T, N_HEADS)
    conv_sample = nh[:, HIST_PAD - HIST:, :].reshape(1, NB, HIST, D_CONV)

    return (y_prompt, y_sample, k_prompt, v_prompt, logf_prompt, conv_prompt,
            k_sample, v_sample, logf_sample, conv_sample)
```

```python
import functools

import jax
import jax.numpy as jnp
import numpy as np
from jax import lax
from jax.experimental import pallas as pl
from jax.experimental.pallas import tpu as pltpu

D_MODEL = 1024
N_HEADS = 8
HEAD_DIM = 64
D_ATTN = N_HEADS * HEAD_DIM
D_CONV = D_MODEL // 2
CONV_WIDTH = 31
HIST = CONV_WIDTH - 1
D_FF = 2816
RMS_EPS = 1e-6
LN_EPS = 1e-5
NEG = -1e30
SCALE = HEAD_DIM ** -0.5

LANES = 128
HIST_PAD = 32
CONV_ROWS = 64
ROW_BLOCK = 128
VMEM_LIMIT = 56 * 1024 * 1024

F32 = jnp.float32
BF16 = jnp.bfloat16


def _dot(a, b):
    return jnp.dot(a, b, preferred_element_type=F32)


def _dot_nt(a, b):
    return lax.dot_general(a, b, (((1,), (1,)), ((), ())), preferred_element_type=F32)


def _const_spec(shape):
    return pl.BlockSpec(shape, lambda *_: (0,) * len(shape), pipeline_mode=pl.Buffered(1))


def _split3(c):
    hi = c.astype(BF16)
    r1 = c - hi.astype(F32)
    mid = r1.astype(BF16)
    lo = (r1 - mid.astype(F32)).astype(BF16)
    return hi, mid, lo


def _inproj_kernel(x_ref, hist_ref, g_ref, wglu_ref, wqkv_ref, wf_ref, bf_ref, wgc_ref, wga_ref,
                   wdw_ref, bdw_ref, lng_ref, lnb_ref, wpw_ref, pq_ref, pk_ref, cq_ref, ck_ref,
                   q_ref, qx_ref, kx_ref, kb_ref, vb_ref, k5_ref, v5_ref, lf_ref, c_ref, sga_ref,
                   cyg_ref, nh_ref, cpad_ref, act_ref, ccar_ref, sgc_ref, *, nb, T, carry):
    R = nb * T
    RB = min(T, CONV_ROWS)
    t = pl.program_id(1)

    xf = x_ref[...]
    ms = jnp.mean(xf * xf, axis=-1, keepdims=True)
    xn = (xf * lax.rsqrt(ms + RMS_EPS) * g_ref[...]).astype(BF16)

    glu = _dot(xn, wglu_ref[...])
    cu = glu[:, :D_CONV] * jax.nn.sigmoid(glu[:, D_CONV:])
    if carry:
        @pl.when(t == 0)
        def _():
            cpad_ref[:, 0:HIST_PAD, :] = hist_ref[...]
            ccar_ref[...] = jnp.zeros_like(ccar_ref)
    else:
        cpad_ref[:, 0:HIST_PAD, :] = hist_ref[...]
    for b in range(nb):
        cpad_ref[b, HIST_PAD:HIST_PAD + T, :] = cu[b * T:(b + 1) * T]

    def qkv_stage():
        qkv = _dot(xn, wqkv_ref[...])
        q_ref[...] = (qkv[:, :D_ATTN] * SCALE).astype(BF16)
        kf = qkv[:, D_ATTN:2 * D_ATTN]
        vf = qkv[:, 2 * D_ATTN:]
        kb_ref[...] = kf.astype(BF16)
        vb_ref[...] = vf.astype(BF16)
        for h in range(N_HEADS):
            hs = slice(h * HEAD_DIM, (h + 1) * HEAD_DIM)
            k5_ref[pl.ds(h, R, stride=N_HEADS), :] = kf[:, hs]
            v5_ref[pl.ds(h, R, stride=N_HEADS), :] = vf[:, hs]

    def attn_gate_stage():
        sga_ref[...] = jax.nn.sigmoid(_dot(xn, wga_ref[...])).astype(BF16)

    def conv_gate_stage():
        sgc_ref[...] = jax.nn.sigmoid(_dot(xn, wgc_ref[...]))

    def forget_stage():
        z = _dot(xn, wf_ref[...]) + bf_ref[...]
        lf = -(jnp.maximum(-z, 0.0) + jnp.log1p(jnp.exp(-jnp.abs(z))))
        lane = lax.broadcasted_iota(jnp.int32, (R, LANES), 1)
        lf = jnp.where(lane < N_HEADS, lf, 0.0)
        lf_ref[...] = lf
        row = lax.broadcasted_iota(jnp.int32, (R, LANES), 0) % T
        c = lf
        d = 1
        while d < T:
            c = c + jnp.where(row >= d, pltpu.roll(c, d, axis=0), 0.0)
            d *= 2
        if carry:
            c = c + ccar_ref[0:1, :]
            ccar_ref[...] = jnp.broadcast_to(c[R - 1:R, :], ccar_ref.shape)
        c_ref[...] = c
        hi, mid, lo = _split3(c)
        qx = _dot(hi, pq_ref[0]) + _dot(mid, pq_ref[1]) + _dot(lo, pq_ref[2]) + cq_ref[...]
        kx = _dot(hi, pk_ref[0]) + _dot(mid, pk_ref[1]) + _dot(lo, pk_ref[2]) + ck_ref[...]
        qx_ref[...] = qx.astype(BF16)
        kx_ref[...] = kx.astype(BF16)

    stages = [qkv_stage, attn_gate_stage, conv_gate_stage, forget_stage]
    blocks = [(b, r0) for b in range(nb) for r0 in range(0, T, RB)]
    per_stage = len(blocks) // len(stages)

    off = HIST_PAD - HIST
    win_rows = RB + HIST_PAD
    for n, (b, r0) in enumerate(blocks):
        tiles = []
        for lt in range(D_CONV // LANES):
            ls = slice(lt * LANES, (lt + 1) * LANES)
            win = cpad_ref[b, r0:r0 + win_rows, ls]
            acc = jnp.broadcast_to(bdw_ref[:, ls], (RB, LANES))
            for res in range(8):
                sh = win if res == 0 else pltpu.roll(win, win_rows - res, axis=0)
                for a in range(HIST_PAD // 8 + 1):
                    w = 8 * a + res - off
                    if 0 <= w < CONV_WIDTH:
                        acc = acc + sh[8 * a:8 * a + RB] * wdw_ref[w:w + 1, ls]
            tiles.append(acc)
        acc = jnp.concatenate(tiles, axis=1)
        mu = jnp.mean(acc, axis=-1, keepdims=True)
        dv = acc - mu
        var = jnp.mean(dv * dv, axis=-1, keepdims=True)
        y = dv * lax.rsqrt(var + LN_EPS) * lng_ref[...] + lnb_ref[...]
        act_ref[b * T + r0:b * T + r0 + RB, :] = (y * jax.nn.sigmoid(y)).astype(BF16)
        if n % per_stage == 0 and stages:
            stages.pop(0)()
    assert not stages

    nh_ref[...] = cpad_ref[:, T:T + HIST_PAD, :]
    if carry:
        cpad_ref[:, 0:HIST_PAD, :] = cpad_ref[:, T:T + HIST_PAD, :]

    cy = _dot(act_ref[...], wpw_ref[...])
    cyg_ref[...] = (sgc_ref[...] * cy).astype(BF16)


def _extra_lane(h):
    pair, odd = divmod(h, 2)
    return pair * LANES + (0 if odd else HEAD_DIM)


def _placement_constants():
    pq = np.zeros((3, LANES, D_ATTN), np.float32)
    pk = np.zeros((3, LANES, D_ATTN), np.float32)
    cq = np.zeros((1, D_ATTN), np.float32)
    ck = np.zeros((1, D_ATTN), np.float32)
    for h in range(N_HEADS):
        p = _extra_lane(h)
        for part in range(3):
            pq[part, h, p + part] = 1.0
            pk[part, h, p + 3 + part] = -1.0
            cq[0, p + 3 + part] = 1.0
            ck[0, p + part] = 1.0
    return pq, pk, cq, ck


def _inproj(x2d, hist, w, *, nb, T, carry):
    n_rows = x2d.shape[0]
    R = nb * T
    n_seg = hist.shape[0]
    n_outer = n_seg // nb
    n_t = n_rows // (R * n_outer)
    grid = (n_outer, n_t)
    row_map = lambda b, t: (b * n_t + t, 0)
    pq, pk, cq, ck = _placement_constants()

    def rows(width, dtype):
        return jax.ShapeDtypeStruct((n_rows, width), dtype), pl.BlockSpec((R, width), row_map)

    def head_rows():
        return (jax.ShapeDtypeStruct((n_rows * N_HEADS, HEAD_DIM), F32),
                pl.BlockSpec((R * N_HEADS, HEAD_DIM), row_map))

    outs = [rows(D_ATTN, BF16), rows(D_ATTN, BF16), rows(D_ATTN, BF16), rows(D_ATTN, BF16),
            rows(D_ATTN, BF16), head_rows(), head_rows(), rows(LANES, F32), rows(LANES, F32),
            rows(D_MODEL, BF16), rows(D_MODEL, BF16)]
    out_shape = [o[0] for o in outs] + [jax.ShapeDtypeStruct((n_seg, HIST_PAD, D_CONV), F32)]
    out_specs = [o[1] for o in outs] + [pl.BlockSpec((nb, HIST_PAD, D_CONV), lambda b, t: (b, 0, 0))]

    consts = [w["g1"], w["wglu"], w["wqkv"], w["wf"], w["bf"], w["wgc"], w["wga"], w["wdw"],
              w["bdw"], w["lng"], w["lnb"], w["wpw"], jnp.asarray(pq, BF16), jnp.asarray(pk, BF16),
              jnp.asarray(cq), jnp.asarray(ck)]
    in_specs = [pl.BlockSpec((R, D_MODEL), row_map),
                pl.BlockSpec((nb, HIST_PAD, D_CONV), lambda b, t: (b, 0, 0))]
    in_specs += [_const_spec(c.shape) for c in consts]

    return pl.pallas_call(
        functools.partial(_inproj_kernel, nb=nb, T=T, carry=carry),
        grid=grid,
        in_specs=in_specs,
        out_specs=out_specs,
        out_shape=out_shape,
        scratch_shapes=[pltpu.VMEM((nb, T + HIST_PAD, D_CONV), F32),
                        pltpu.VMEM((R, D_CONV), BF16),
                        pltpu.VMEM((8, LANES), F32),
                        pltpu.VMEM((R, D_MODEL), F32)],
        compiler_params=pltpu.CompilerParams(
            dimension_semantics=("arbitrary", "arbitrary"), vmem_limit_bytes=VMEM_LIMIT),
        name="inproj_conv",
    )(x2d, hist, *consts)


def _attn_prompt_kernel(q_ref, qx_ref, k_ref, kx_ref, v_ref, o_ref, ka_ref, va_ref, *, tq, tk, tkd, hps):
    i = pl.program_id(2)
    nh = 2 * hps
    lane = lax.broadcasted_iota(jnp.int32, (tq, LANES), 1)
    low = lane < HEAD_DIM
    row0 = pl.multiple_of(i * tq, tq)
    one_hi = (lane == HEAD_DIM).astype(BF16)
    one_lo = (lane == 0).astype(BF16)

    qa = []
    for hp in range(hps):
        ps = slice(hp * LANES, (hp + 1) * LANES)
        ev = slice(2 * hp * LANES, (2 * hp + 1) * LANES)
        od = slice((2 * hp + 1) * LANES, (2 * hp + 2) * LANES)
        q, qx, k, kx, v = (r[0, :, ps] for r in (q_ref, qx_ref, k_ref, kx_ref, v_ref))
        ka_ref[pl.ds(row0, tq), ev] = jnp.where(low, k, kx)
        ka_ref[pl.ds(row0, tq), od] = jnp.where(low, kx, k)
        va_ref[pl.ds(row0, tq), ev] = jnp.where(low, v, one_hi)
        va_ref[pl.ds(row0, tq), od] = jnp.where(low, one_lo, v)
        qa += [jnp.where(low, q, qx), jnp.where(low, qx, q)]

    def step(col0, width, carry, r_lo=None):
        masked = r_lo is not None
        r_lo = r_lo or 0
        out = []
        for e in range(nh):
            m, acc = carry[e]
            kj = ka_ref[pl.ds(col0, width), e * LANES:(e + 1) * LANES]
            vj = va_ref[pl.ds(col0, width), e * LANES:(e + 1) * LANES]
            s = _dot_nt(qa[e][r_lo:], kj)
            if masked:
                r_id = lax.broadcasted_iota(jnp.int32, (tq - r_lo, width), 0)
                c_id = lax.broadcasted_iota(jnp.int32, (tq - r_lo, width), 1)
                s = jnp.where(c_id <= r_id, s, NEG)
            m_new = jnp.maximum(m[r_lo:], jnp.max(s, axis=-1, keepdims=True))
            alpha = jnp.exp(m[r_lo:] - m_new)
            p = jnp.exp(s - m_new).astype(BF16)
            acc_new = alpha * acc[r_lo:] + _dot(p, vj)
            if r_lo:
                m_new = jnp.concatenate([m[:r_lo], m_new], axis=0)
                acc_new = jnp.concatenate([acc[:r_lo], acc_new], axis=0)
            out.append((m_new, acc_new))
        return tuple(out)

    init = tuple((jnp.full((tq, 1), NEG, F32), jnp.zeros((tq, LANES), F32)) for _ in range(nh))
    carry = lax.fori_loop(0, i * (tq // tk),
                          lambda j, c: step(pl.multiple_of(j * tk, tk), tk, c), init)
    for d in range(tq // tkd):
        carry = step(pl.multiple_of(row0 + d * tkd, tkd), tkd, carry, r_lo=d * tkd)
    for hp in range(hps):
        acc0, acc1 = carry[2 * hp][1], carry[2 * hp + 1][1]
        o0 = acc0 / acc0[:, HEAD_DIM:HEAD_DIM + 1]
        o1 = acc1 / acc1[:, 0:1]
        o_ref[0, :, hp * LANES:(hp + 1) * LANES] = jnp.where(low, o0, o1).astype(BF16)


def _attn_prompt(q, qx, k, kx, v, *, batch, seq, tq, tk, tkd, hps):
    shp = (batch, seq, D_ATTN)
    q, qx, k, kx, v = (a.reshape(shp) for a in (q, qx, k, kx, v))
    spec = pl.BlockSpec((1, tq, hps * LANES), lambda b, g, i: (b, i, g))
    return pl.pallas_call(
        functools.partial(_attn_prompt_kernel, tq=tq, tk=tk, tkd=tkd, hps=hps),
        grid=(batch, N_HEADS // (2 * hps), seq // tq),
        in_specs=[spec] * 5,
        out_specs=spec,
        out_shape=jax.ShapeDtypeStruct(shp, BF16),
        scratch_shapes=[pltpu.VMEM((seq, 2 * hps * LANES), BF16),
                        pltpu.VMEM((seq, 2 * hps * LANES), BF16)],
        compiler_params=pltpu.CompilerParams(
            dimension_semantics=("arbitrary", "arbitrary", "arbitrary"),
            vmem_limit_bytes=VMEM_LIMIT),
        name="attn_prompt",
    )(q, qx, k, kx, v).reshape(batch * seq, D_ATTN)


def _suffix_sum_kernel(x_ref, o_ref):
    x = x_ref[...]
    n = x.shape[1]
    lane = lax.broadcasted_iota(jnp.int32, x.shape, 1)
    s = x
    d = 1
    while d < n:
        s = s + jnp.where(lane < n - d, pltpu.roll(s, n - d, axis=1), 0.0)
        d *= 2
    o_ref[...] = s - x


def _suffix_sum(x):
    return pl.pallas_call(
        _suffix_sum_kernel,
        out_shape=jax.ShapeDtypeStruct(x.shape, F32),
        compiler_params=pltpu.CompilerParams(vmem_limit_bytes=VMEM_LIMIT),
        name="cache_decay",
    )(x)


def _attn_sample_kernel(q_ref, ck_ref, cv_ref, r_ref, kn_ref, vn_ref, rn_ref, o_ref,
                        m_ref, l_ref, acc_ref, *, nq):
    j = pl.program_id(1)
    heads = range(N_HEADS)
    rows = [slice(h * nq, (h + 1) * nq) for h in heads]
    lanes = [slice(h * HEAD_DIM, (h + 1) * HEAD_DIM) for h in heads]

    def stack(parts):
        return jnp.concatenate(parts, axis=0)

    def update(scores, decay, pv, mask=None):
        n = decay.shape[1]
        s = stack(scores) + stack([jnp.broadcast_to(decay[h:h + 1, :], (nq, n)) for h in heads])
        if mask is not None:
            s = jnp.where(mask, s, NEG)
        m_old = m_ref[:, 0:1]
        m_new = jnp.maximum(m_old, jnp.max(s, axis=-1, keepdims=True))
        alpha = jnp.exp(m_old - m_new)
        p = jnp.exp(s - m_new)
        l_new = alpha * l_ref[:, 0:1] + jnp.sum(p, axis=-1, keepdims=True)
        p = p.astype(BF16)
        acc_ref[...] = alpha * acc_ref[...] + stack([pv(h, p[rows[h]]) for h in heads])
        m_ref[...] = jnp.broadcast_to(m_new, m_ref.shape)
        l_ref[...] = jnp.broadcast_to(l_new, l_ref.shape)

    @pl.when(j == 0)
    def _():
        m_ref[...] = jnp.full(m_ref.shape, NEG, F32)
        l_ref[...] = jnp.zeros_like(l_ref)
        acc_ref[...] = jnp.zeros_like(acc_ref)
        q_id = lax.broadcasted_iota(jnp.int32, (N_HEADS * nq, nq), 0) % nq
        k_id = lax.broadcasted_iota(jnp.int32, (N_HEADS * nq, nq), 1)
        kn = kn_ref[0]
        vn = vn_ref[0]
        update([_dot_nt(q_ref[0, h], kn[:, lanes[h]]) for h in heads], rn_ref[0],
               lambda h, p: _dot(p, vn[:, lanes[h]]), mask=k_id <= q_id)

    update([_dot(q_ref[0, h], ck_ref[0, h].astype(BF16)) for h in heads], r_ref[0],
           lambda h, p: _dot_nt(p, cv_ref[0, h].astype(BF16)))

    @pl.when(j == pl.num_programs(1) - 1)
    def _():
        o = (acc_ref[...] / l_ref[:, 0:1]).astype(BF16)
        for h in heads:
            o_ref[0, :, lanes[h]] = o[rows[h]]


def _attn_sample(qh, cache_kt, cache_vt, r_cache, k_new, v_new, r_new, *, tk):
    nb, _, _, past = cache_kt.shape
    nq = k_new.shape[1]
    cache_spec = pl.BlockSpec((1, N_HEADS, HEAD_DIM, tk), lambda b, j: (b, 0, 0, j))
    return pl.pallas_call(
        functools.partial(_attn_sample_kernel, nq=nq),
        grid=(nb, past // tk),
        in_specs=[pl.BlockSpec((1, N_HEADS, nq, HEAD_DIM), lambda b, j: (b, 0, 0, 0)),
                  cache_spec, cache_spec,
                  pl.BlockSpec((1, N_HEADS, tk), lambda b, j: (b, 0, j)),
                  pl.BlockSpec((1, nq, D_ATTN), lambda b, j: (b, 0, 0)),
                  pl.BlockSpec((1, nq, D_ATTN), lambda b, j: (b, 0, 0)),
                  pl.BlockSpec((1, N_HEADS, nq), lambda b, j: (b, 0, 0))],
        out_specs=pl.BlockSpec((1, nq, D_ATTN), lambda b, j: (b, 0, 0)),
        out_shape=jax.ShapeDtypeStruct((nb, nq, D_ATTN), BF16),
        scratch_shapes=[pltpu.VMEM((N_HEADS * nq, LANES), F32),
                        pltpu.VMEM((N_HEADS * nq, LANES), F32),
                        pltpu.VMEM((N_HEADS * nq, HEAD_DIM), F32)],
        compiler_params=pltpu.CompilerParams(
            dimension_semantics=("arbitrary", "arbitrary"), vmem_limit_bytes=VMEM_LIMIT),
        name="attn_sample",
    )(qh, cache_kt, cache_vt, r_cache, k_new, v_new, r_new)


def _rms(x, g):
    return x * lax.rsqrt(jnp.mean(x * x, axis=-1, keepdims=True) + RMS_EPS) * g


def _out_ffn_kernel(x_ref, ao_ref, sga_ref, cyg_ref, wao_ref, wout_ref, g2_ref, wg_ref, wu_ref,
                    wd_ref, gf_ref, y_ref):
    ap = _dot(ao_ref[...], wao_ref[...])
    mixed = cyg_ref[...].astype(F32) + sga_ref[...].astype(F32) * ap
    h = x_ref[...] + _dot(mixed.astype(BF16), wout_ref[...])
    z = _rms(h, g2_ref[...]).astype(BF16)
    gate = _dot(z, wg_ref[...])
    act = (gate * jax.nn.sigmoid(gate) * _dot(z, wu_ref[...])).astype(BF16)
    h = h + _dot(act, wd_ref[...])
    y_ref[...] = _rms(h, gf_ref[...])


def _out_ffn(x2d, ao, sga, cyg, w, *, R):
    n_rows = x2d.shape[0]
    row_map = lambda i: (i, 0)
    consts = [w["wao"], w["wout"], w["g2"], w["wg"], w["wu"], w["wd"], w["gf"]]
    in_specs = [pl.BlockSpec((R, D_MODEL), row_map), pl.BlockSpec((R, D_ATTN), row_map),
                pl.BlockSpec((R, D_MODEL), row_map), pl.BlockSpec((R, D_MODEL), row_map)]
    in_specs += [_const_spec(c.shape) for c in consts]
    return pl.pallas_call(
        _out_ffn_kernel,
        grid=(n_rows // R,),
        in_specs=in_specs,
        out_specs=pl.BlockSpec((R, D_MODEL), row_map),
        out_shape=jax.ShapeDtypeStruct((n_rows, D_MODEL), F32),
        compiler_params=pltpu.CompilerParams(
            dimension_semantics=("arbitrary",), vmem_limit_bytes=VMEM_LIMIT),
        name="out_ffn",
    )(x2d, ao, sga, cyg, *consts)


def _prep_weights(norm_mix_g, w_in, b_f, w_dw, b_dw, ln_g, ln_b, w_conv_pw, w_attn_o, w_out,
                  norm_ffn_g, w_gate, w_up, w_down, final_norm_g):
    o_q = 2 * D_CONV
    o_f = o_q + 3 * D_ATTN
    o_gc = o_f + N_HEADS
    o_ga = o_gc + D_MODEL
    row = lambda a: a.reshape(1, -1).astype(F32)
    return {
        "g1": row(norm_mix_g),
        "wglu": w_in[:, :o_q].astype(BF16),
        "wqkv": w_in[:, o_q:o_f].astype(BF16),
        "wf": jnp.pad(w_in[:, o_f:o_gc], ((0, 0), (0, LANES - N_HEADS))).astype(BF16),
        "bf": jnp.pad(row(b_f), ((0, 0), (0, LANES - N_HEADS))),
        "wgc": w_in[:, o_gc:o_ga].astype(BF16),
        "wga": w_in[:, o_ga:].astype(BF16),
        "wdw": jnp.pad(w_dw.astype(F32), ((0, HIST_PAD - CONV_WIDTH), (0, 0))),
        "bdw": row(b_dw), "lng": row(ln_g), "lnb": row(ln_b),
        "wpw": w_conv_pw.astype(BF16),
        "wao": w_attn_o.astype(BF16), "wout": w_out.astype(BF16), "g2": row(norm_ffn_g),
        "wg": w_gate.astype(BF16), "wu": w_up.astype(BF16), "wd": w_down.astype(BF16),
        "gf": row(final_norm_g),
    }


def kernel(x_prompt, x_sample, cache_k, cache_v, cache_logf, state_conv, norm_mix_g, w_in, b_f,
           w_dw, b_dw, ln_g, ln_b, w_conv_pw, w_attn_o, w_out, norm_ffn_g, w_gate, w_up, w_down,
           final_norm_g):
    B, S, _ = x_prompt.shape
    NB, T, _ = x_sample.shape
    P = cache_k.shape[2]
    w = _prep_weights(norm_mix_g[0], w_in[0], b_f[0], w_dw[0], b_dw[0], ln_g[0], ln_b[0],
                      w_conv_pw[0], w_attn_o[0], w_out[0], norm_ffn_g[0], w_gate[0], w_up[0],
                      w_down[0], final_norm_g)

    xp = x_prompt.reshape(B * S, D_MODEL)
    hist0 = jnp.zeros((B, HIST_PAD, D_CONV), F32)
    q, qx, kx, kb, vb, k5, v5, lf, _, sga, cyg, nh = _inproj(xp, hist0, w, nb=1, T=256, carry=True)
    ao = _attn_prompt(q, qx, kb, kx, vb, batch=B, seq=S, tq=1024, tk=1024, tkd=512, hps=1)
    y_prompt = _out_ffn(xp, ao, sga, cyg, w, R=256).reshape(B, S, D_MODEL)
    k_prompt = k5.reshape(1, B, S, N_HEADS, HEAD_DIM)
    v_prompt = v5.reshape(1, B, S, N_HEADS, HEAD_DIM)
    logf_prompt = lf[:, :N_HEADS].reshape(1, B, S, N_HEADS)
    conv_prompt = nh[:, HIST_PAD - HIST:, :].reshape(1, B, HIST, D_CONV)

    xs = x_sample.reshape(NB * T, D_MODEL)
    hist_s = jnp.pad(state_conv[0].astype(F32), ((0, 0), (HIST_PAD - HIST, 0), (0, 0)))
    q, _, _, kb, vb, k5, v5, lf, c, sga, cyg, nh = _inproj(xs, hist_s, w, nb=NB, T=T, carry=False)
    qh = jnp.transpose(q.reshape(NB, T, N_HEADS, HEAD_DIM), (0, 2, 1, 3))
    r_new = -jnp.transpose(c[:, :N_HEADS].reshape(NB, T, N_HEADS), (0, 2, 1))
    clf_t = jnp.transpose(cache_logf[0].astype(F32), (0, 2, 1)).reshape(NB * N_HEADS, P)
    r_cache = _suffix_sum(clf_t).reshape(NB, N_HEADS, P)
    ao = _attn_sample(qh, jnp.transpose(cache_k[0], (0, 2, 3, 1)),
                      jnp.transpose(cache_v[0], (0, 2, 3, 1)), r_cache,
                      kb.reshape(NB, T, D_ATTN), vb.reshape(NB, T, D_ATTN), r_new, tk=P)
    y_sample = _out_ffn(xs, ao.reshape(NB * T, D_ATTN), sga, cyg, w, R=NB * T)
    y_sample = y_sample.reshape(NB, T, D_MODEL)
    k_sample = k5.reshape(1, NB, T, N_HEADS, HEAD_DIM)
    v_sample = v5.reshape(1, NB, T, N_HEADS, HEAD_DIM)
    logf_sample = lf[:, :N_HEADS].reshape(1, NB, T, N_HEADS)
    conv_sample = nh[:, HIST_PAD - HIST:, :].reshape(1, NB, HIST, D_CONV)

    return (y_prompt, y_sample, k_prompt, v_prompt, logf_prompt, conv_prompt,
            k_sample, v_sample, logf_sample, conv_sample)
```

```python
import functools

import jax
import jax.numpy as jnp
import numpy as np
from jax import lax
from jax.experimental import pallas as pl
from jax.experimental.pallas import tpu as pltpu

D_MODEL = 1024
N_HEADS = 8
HEAD_DIM = 64
D_ATTN = N_HEADS * HEAD_DIM
D_CONV = D_MODEL // 2
CONV_WIDTH = 31
HIST = CONV_WIDTH - 1
D_FF = 2816
RMS_EPS = 1e-6
LN_EPS = 1e-5
NEG = -1e30
SCALE = HEAD_DIM ** -0.5

LANES = 128
HIST_PAD = 32
CONV_ROWS = 64
ROW_BLOCK = 128
VMEM_LIMIT = 56 * 1024 * 1024

F32 = jnp.float32
BF16 = jnp.bfloat16


def _dot(a, b):
    return jnp.dot(a, b, preferred_element_type=F32)


def _dot_nt(a, b):
    return lax.dot_general(a, b, (((1,), (1,)), ((), ())), preferred_element_type=F32)


def _const_spec(shape):
    return pl.BlockSpec(shape, lambda *_: (0,) * len(shape), pipeline_mode=pl.Buffered(1))


def _split3(c):
    hi = c.astype(BF16)
    r1 = c - hi.astype(F32)
    mid = r1.astype(BF16)
    lo = (r1 - mid.astype(F32)).astype(BF16)
    return hi, mid, lo


def _inproj_kernel(x_ref, hist_ref, g_ref, wglu_ref, wqkv_ref, wf_ref, bf_ref, wgc_ref, wga_ref,
                   wdw_ref, bdw_ref, lng_ref, lnb_ref, wpw_ref, pq_ref, pk_ref, cq_ref, ck_ref,
                   q_ref, qx_ref, kx_ref, kb_ref, vb_ref, k5_ref, v5_ref, lf_ref, c_ref, sga_ref,
                   cyg_ref, nh_ref, cpad_ref, act_ref, ccar_ref, sgc_ref, *, nb, T, carry):
    R = nb * T
    RB = min(T, CONV_ROWS)
    t = pl.program_id(1)

    xf = x_ref[...]
    ms = jnp.mean(xf * xf, axis=-1, keepdims=True)
    xn = (xf * lax.rsqrt(ms + RMS_EPS) * g_ref[...]).astype(BF16)

    glu = _dot(xn, wglu_ref[...])
    cu = glu[:, :D_CONV] * jax.nn.sigmoid(glu[:, D_CONV:])
    if carry:
        @pl.when(t == 0)
        def _():
            cpad_ref[:, 0:HIST_PAD, :] = hist_ref[...]
            ccar_ref[...] = jnp.zeros_like(ccar_ref)
    else:
        cpad_ref[:, 0:HIST_PAD, :] = hist_ref[...]
    for b in range(nb):
        cpad_ref[b, HIST_PAD:HIST_PAD + T, :] = cu[b * T:(b + 1) * T]

    def qkv_stage():
        qkv = _dot(xn, wqkv_ref[...])
        q_ref[...] = (qkv[:, :D_ATTN] * SCALE).astype(BF16)
        kf = qkv[:, D_ATTN:2 * D_ATTN]
        vf = qkv[:, 2 * D_ATTN:]
        kb_ref[...] = kf.astype(BF16)
        vb_ref[...] = vf.astype(BF16)
        for h in range(N_HEADS):
            hs = slice(h * HEAD_DIM, (h + 1) * HEAD_DIM)
            k5_ref[pl.ds(h, R, stride=N_HEADS), :] = kf[:, hs]
            v5_ref[pl.ds(h, R, stride=N_HEADS), :] = vf[:, hs]

    def attn_gate_stage():
        sga_ref[...] = jax.nn.sigmoid(_dot(xn, wga_ref[...])).astype(BF16)

    def conv_gate_stage():
        sgc_ref[...] = jax.nn.sigmoid(_dot(xn, wgc_ref[...]))

    def forget_stage():
        z = _dot(xn, wf_ref[...]) + bf_ref[...]
        lf = -(jnp.maximum(-z, 0.0) + jnp.log1p(jnp.exp(-jnp.abs(z))))
        lane = lax.broadcasted_iota(jnp.int32, (R, LANES), 1)
        lf = jnp.where(lane < N_HEADS, lf, 0.0)
        lf_ref[0] = lf.T[:N_HEADS]
        row = lax.broadcasted_iota(jnp.int32, (R, LANES), 0) % T
        c = lf
        d = 1
        while d < T:
            c = c + jnp.where(row >= d, pltpu.roll(c, d, axis=0), 0.0)
            d *= 2
        if carry:
            c = c + ccar_ref[0:1, :]
            ccar_ref[...] = jnp.broadcast_to(c[R - 1:R, :], ccar_ref.shape)
        c_ref[0] = c.T[:N_HEADS]
        hi, mid, lo = _split3(c)
        qx = _dot(hi, pq_ref[0]) + _dot(mid, pq_ref[1]) + _dot(lo, pq_ref[2]) + cq_ref[...]
        kx = _dot(hi, pk_ref[0]) + _dot(mid, pk_ref[1]) + _dot(lo, pk_ref[2]) + ck_ref[...]
        qx_ref[...] = qx.astype(BF16)
        kx_ref[...] = kx.astype(BF16)

    stages = [qkv_stage, attn_gate_stage, conv_gate_stage, forget_stage]
    blocks = [(b, r0) for b in range(nb) for r0 in range(0, T, RB)]
    per_stage = len(blocks) // len(stages)

    off = HIST_PAD - HIST
    win_rows = RB + HIST_PAD
    for n, (b, r0) in enumerate(blocks):
        tiles = []
        for lt in range(D_CONV // LANES):
            ls = slice(lt * LANES, (lt + 1) * LANES)
            win = cpad_ref[b, r0:r0 + win_rows, ls]
            acc = jnp.broadcast_to(bdw_ref[:, ls], (RB, LANES))
            for res in range(8):
                sh = win if res == 0 else pltpu.roll(win, win_rows - res, axis=0)
                for a in range(HIST_PAD // 8 + 1):
                    w = 8 * a + res - off
                    if 0 <= w < CONV_WIDTH:
                        acc = acc + sh[8 * a:8 * a + RB] * wdw_ref[w:w + 1, ls]
            tiles.append(acc)
        acc = jnp.concatenate(tiles, axis=1)
        mu = jnp.mean(acc, axis=-1, keepdims=True)
        dv = acc - mu
        var = jnp.mean(dv * dv, axis=-1, keepdims=True)
        y = dv * lax.rsqrt(var + LN_EPS) * lng_ref[...] + lnb_ref[...]
        act_ref[b * T + r0:b * T + r0 + RB, :] = (y * jax.nn.sigmoid(y)).astype(BF16)
        if n % per_stage == 0 and stages:
            stages.pop(0)()
    assert not stages

    nh_ref[...] = cpad_ref[:, T:T + HIST_PAD, :]
    if carry:
        cpad_ref[:, 0:HIST_PAD, :] = cpad_ref[:, T:T + HIST_PAD, :]

    cy = _dot(act_ref[...], wpw_ref[...])
    cyg_ref[...] = (sgc_ref[...] * cy).astype(BF16)


def _extra_lane(h):
    pair, odd = divmod(h, 2)
    return pair * LANES + (0 if odd else HEAD_DIM)


def _placement_constants():
    pq = np.zeros((3, LANES, D_ATTN), np.float32)
    pk = np.zeros((3, LANES, D_ATTN), np.float32)
    cq = np.zeros((1, D_ATTN), np.float32)
    ck = np.zeros((1, D_ATTN), np.float32)
    for h in range(N_HEADS):
        p = _extra_lane(h)
        for part in range(3):
            pq[part, h, p + part] = 1.0
            pk[part, h, p + 3 + part] = -1.0
            cq[0, p + 3 + part] = 1.0
            ck[0, p + part] = 1.0
    return pq, pk, cq, ck


def _inproj(x2d, hist, w, *, nb, T, carry):
    n_rows = x2d.shape[0]
    R = nb * T
    n_seg = hist.shape[0]
    n_outer = n_seg // nb
    n_t = n_rows // (R * n_outer)
    grid = (n_outer, n_t)
    row_map = lambda b, t: (b * n_t + t, 0)
    pq, pk, cq, ck = _placement_constants()

    def rows(width, dtype):
        return jax.ShapeDtypeStruct((n_rows, width), dtype), pl.BlockSpec((R, width), row_map)

    def head_rows():
        return (jax.ShapeDtypeStruct((n_rows * N_HEADS, HEAD_DIM), F32),
                pl.BlockSpec((R * N_HEADS, HEAD_DIM), row_map))

    def head_major():
        return (jax.ShapeDtypeStruct((n_outer, N_HEADS, n_t * R), F32),
                pl.BlockSpec((1, N_HEADS, R), lambda b, t: (b, 0, t)))

    outs = [rows(D_ATTN, BF16), rows(D_ATTN, BF16), rows(D_ATTN, BF16), rows(D_ATTN, BF16),
            rows(D_ATTN, BF16), head_rows(), head_rows(), head_major(), head_major(),
            rows(D_MODEL, BF16), rows(D_MODEL, BF16)]
    out_shape = [o[0] for o in outs] + [jax.ShapeDtypeStruct((n_seg, HIST_PAD, D_CONV), F32)]
    out_specs = [o[1] for o in outs] + [pl.BlockSpec((nb, HIST_PAD, D_CONV), lambda b, t: (b, 0, 0))]

    consts = [w["g1"], w["wglu"], w["wqkv"], w["wf"], w["bf"], w["wgc"], w["wga"], w["wdw"],
              w["bdw"], w["lng"], w["lnb"], w["wpw"], jnp.asarray(pq, BF16), jnp.asarray(pk, BF16),
              jnp.asarray(cq), jnp.asarray(ck)]
    in_specs = [pl.BlockSpec((R, D_MODEL), row_map),
                pl.BlockSpec((nb, HIST_PAD, D_CONV), lambda b, t: (b, 0, 0))]
    in_specs += [_const_spec(c.shape) for c in consts]

    return pl.pallas_call(
        functools.partial(_inproj_kernel, nb=nb, T=T, carry=carry),
        grid=grid,
        in_specs=in_specs,
        out_specs=out_specs,
        out_shape=out_shape,
        scratch_shapes=[pltpu.VMEM((nb, T + HIST_PAD, D_CONV), F32),
                        pltpu.VMEM((R, D_CONV), BF16),
                        pltpu.VMEM((8, LANES), F32),
                        pltpu.VMEM((R, D_MODEL), F32)],
        compiler_params=pltpu.CompilerParams(
            dimension_semantics=("arbitrary", "arbitrary"), vmem_limit_bytes=VMEM_LIMIT),
        name="inproj_conv",
    )(x2d, hist, *consts)


def _attn_prompt_kernel(q_ref, qx_ref, k_ref, kx_ref, v_ref, o_ref, ka_ref, va_ref, *, tq, tk, tkd, hps):
    i = pl.program_id(2)
    nh = 2 * hps
    lane = lax.broadcasted_iota(jnp.int32, (tq, LANES), 1)
    low = lane < HEAD_DIM
    row0 = pl.multiple_of(i * tq, tq)
    one_hi = (lane == HEAD_DIM).astype(BF16)
    one_lo = (lane == 0).astype(BF16)

    qa = []
    for hp in range(hps):
        ps = slice(hp * LANES, (hp + 1) * LANES)
        ev = slice(2 * hp * LANES, (2 * hp + 1) * LANES)
        od = slice((2 * hp + 1) * LANES, (2 * hp + 2) * LANES)
        q, qx, k, kx, v = (r[0, :, ps] for r in (q_ref, qx_ref, k_ref, kx_ref, v_ref))
        ka_ref[pl.ds(row0, tq), ev] = jnp.where(low, k, kx)
        ka_ref[pl.ds(row0, tq), od] = jnp.where(low, kx, k)
        va_ref[pl.ds(row0, tq), ev] = jnp.where(low, v, one_hi)
        va_ref[pl.ds(row0, tq), od] = jnp.where(low, one_lo, v)
        qa += [jnp.where(low, q, qx), jnp.where(low, qx, q)]

    def step(col0, width, carry, r_lo=None):
        masked = r_lo is not None
        r_lo = r_lo or 0
        out = []
        for e in range(nh):
            m, acc = carry[e]
            kj = ka_ref[pl.ds(col0, width), e * LANES:(e + 1) * LANES]
            vj = va_ref[pl.ds(col0, width), e * LANES:(e + 1) * LANES]
            s = _dot_nt(qa[e][r_lo:], kj)
            if masked:
                r_id = lax.broadcasted_iota(jnp.int32, (tq - r_lo, width), 0)
                c_id = lax.broadcasted_iota(jnp.int32, (tq - r_lo, width), 1)
                s = jnp.where(c_id <= r_id, s, NEG)
            m_new = jnp.maximum(m[r_lo:], jnp.max(s, axis=-1, keepdims=True))
            alpha = jnp.exp(m[r_lo:] - m_new)
            p = jnp.exp(s - m_new).astype(BF16)
            acc_new = alpha * acc[r_lo:] + _dot(p, vj)
            if r_lo:
                m_new = jnp.concatenate([m[:r_lo], m_new], axis=0)
                acc_new = jnp.concatenate([acc[:r_lo], acc_new], axis=0)
            out.append((m_new, acc_new))
        return tuple(out)

    init = tuple((jnp.full((tq, 1), NEG, F32), jnp.zeros((tq, LANES), F32)) for _ in range(nh))
    carry = lax.fori_loop(0, i * (tq // tk),
                          lambda j, c: step(pl.multiple_of(j * tk, tk), tk, c), init)
    for d in range(tq // tkd):
        carry = step(pl.multiple_of(row0 + d * tkd, tkd), tkd, carry, r_lo=d * tkd)
    for hp in range(hps):
        acc0, acc1 = carry[2 * hp][1], carry[2 * hp + 1][1]
        o0 = acc0 / acc0[:, HEAD_DIM:HEAD_DIM + 1]
        o1 = acc1 / acc1[:, 0:1]
        o_ref[0, :, hp * LANES:(hp + 1) * LANES] = jnp.where(low, o0, o1).astype(BF16)


def _attn_prompt(q, qx, k, kx, v, *, batch, seq, tq, tk, tkd, hps):
    shp = (batch, seq, D_ATTN)
    q, qx, k, kx, v = (a.reshape(shp) for a in (q, qx, k, kx, v))
    spec = pl.BlockSpec((1, tq, hps * LANES), lambda b, g, i: (b, i, g))
    return pl.pallas_call(
        functools.partial(_attn_prompt_kernel, tq=tq, tk=tk, tkd=tkd, hps=hps),
        grid=(batch, N_HEADS // (2 * hps), seq // tq),
        in_specs=[spec] * 5,
        out_specs=spec,
        out_shape=jax.ShapeDtypeStruct(shp, BF16),
        scratch_shapes=[pltpu.VMEM((seq, 2 * hps * LANES), BF16),
                        pltpu.VMEM((seq, 2 * hps * LANES), BF16)],
        compiler_params=pltpu.CompilerParams(
            dimension_semantics=("arbitrary", "arbitrary", "arbitrary"),
            vmem_limit_bytes=VMEM_LIMIT),
        name="attn_prompt",
    )(q, qx, k, kx, v).reshape(batch * seq, D_ATTN)


def _suffix_sum_kernel(x_ref, o_ref):
    x = x_ref[...]
    n = x.shape[1]
    lane = lax.broadcasted_iota(jnp.int32, x.shape, 1)
    s = x
    d = 1
    while d < n:
        s = s + jnp.where(lane < n - d, pltpu.roll(s, n - d, axis=1), 0.0)
        d *= 2
    o_ref[...] = s - x


def _suffix_sum(x):
    return pl.pallas_call(
        _suffix_sum_kernel,
        out_shape=jax.ShapeDtypeStruct(x.shape, F32),
        compiler_params=pltpu.CompilerParams(vmem_limit_bytes=VMEM_LIMIT),
        name="cache_decay",
    )(x)


def _attn_sample_kernel(q_ref, ck_ref, cv_ref, r_ref, kn_ref, vn_ref, rn_ref, o_ref,
                        m_ref, l_ref, acc_ref, *, nq):
    j = pl.program_id(1)
    heads = range(N_HEADS)
    rows = [slice(h * nq, (h + 1) * nq) for h in heads]
    lanes = [slice(h * HEAD_DIM, (h + 1) * HEAD_DIM) for h in heads]
    q = q_ref[0]

    def stack(parts):
        return jnp.concatenate(parts, axis=0)

    def update(scores, decay, pv, mask=None):
        n = decay.shape[1]
        s = stack(scores) + stack([jnp.broadcast_to(decay[h:h + 1, :], (nq, n)) for h in heads])
        if mask is not None:
            s = jnp.where(mask, s, NEG)
        m_old = m_ref[:, 0:1]
        m_new = jnp.maximum(m_old, jnp.max(s, axis=-1, keepdims=True))
        alpha = jnp.exp(m_old - m_new)
        p = jnp.exp(s - m_new)
        l_new = alpha * l_ref[:, 0:1] + jnp.sum(p, axis=-1, keepdims=True)
        p = p.astype(BF16)
        acc_ref[...] = alpha * acc_ref[...] + stack([pv(h, p[rows[h]]) for h in heads])
        m_ref[...] = jnp.broadcast_to(m_new, m_ref.shape)
        l_ref[...] = jnp.broadcast_to(l_new, l_ref.shape)

    @pl.when(j == 0)
    def _():
        m_ref[...] = jnp.full(m_ref.shape, NEG, F32)
        l_ref[...] = jnp.zeros_like(l_ref)
        acc_ref[...] = jnp.zeros_like(acc_ref)
        q_id = lax.broadcasted_iota(jnp.int32, (N_HEADS * nq, nq), 0) % nq
        k_id = lax.broadcasted_iota(jnp.int32, (N_HEADS * nq, nq), 1)
        kn = kn_ref[0]
        vn = vn_ref[0]
        update([_dot_nt(q[:, lanes[h]], kn[:, lanes[h]]) for h in heads], rn_ref[0],
               lambda h, p: _dot(p, vn[:, lanes[h]]), mask=k_id <= q_id)

    update([_dot(q[:, lanes[h]], ck_ref[0, h].astype(BF16)) for h in heads], r_ref[0],
           lambda h, p: _dot_nt(p, cv_ref[0, h].astype(BF16)))

    @pl.when(j == pl.num_programs(1) - 1)
    def _():
        o = (acc_ref[...] / l_ref[:, 0:1]).astype(BF16)
        for h in heads:
            o_ref[0, :, lanes[h]] = o[rows[h]]


def _attn_sample(qh, cache_kt, cache_vt, r_cache, k_new, v_new, r_new, *, tk):
    nb, _, _, past = cache_kt.shape
    nq = k_new.shape[1]
    cache_spec = pl.BlockSpec((1, N_HEADS, HEAD_DIM, tk), lambda b, j: (b, 0, 0, j))
    return pl.pallas_call(
        functools.partial(_attn_sample_kernel, nq=nq),
        grid=(nb, past // tk),
        in_specs=[pl.BlockSpec((1, nq, D_ATTN), lambda b, j: (b, 0, 0)),
                  cache_spec, cache_spec,
                  pl.BlockSpec((1, N_HEADS, tk), lambda b, j: (b, 0, j)),
                  pl.BlockSpec((1, nq, D_ATTN), lambda b, j: (b, 0, 0)),
                  pl.BlockSpec((1, nq, D_ATTN), lambda b, j: (b, 0, 0)),
                  pl.BlockSpec((1, N_HEADS, nq), lambda b, j: (b, 0, 0))],
        out_specs=pl.BlockSpec((1, nq, D_ATTN), lambda b, j: (b, 0, 0)),
        out_shape=jax.ShapeDtypeStruct((nb, nq, D_ATTN), BF16),
        scratch_shapes=[pltpu.VMEM((N_HEADS * nq, LANES), F32),
                        pltpu.VMEM((N_HEADS * nq, LANES), F32),
                        pltpu.VMEM((N_HEADS * nq, HEAD_DIM), F32)],
        compiler_params=pltpu.CompilerParams(
            dimension_semantics=("arbitrary", "arbitrary"), vmem_limit_bytes=VMEM_LIMIT),
        name="attn_sample",
    )(qh, cache_kt, cache_vt, r_cache, k_new, v_new, r_new)


def _rms(x, g):
    return x * lax.rsqrt(jnp.mean(x * x, axis=-1, keepdims=True) + RMS_EPS) * g


def _out_ffn_kernel(x_ref, ao_ref, sga_ref, cyg_ref, wao_ref, wout_ref, g2_ref, wg_ref, wu_ref,
                    wd_ref, gf_ref, y_ref):
    ap = _dot(ao_ref[...], wao_ref[...])
    mixed = cyg_ref[...].astype(F32) + sga_ref[...].astype(F32) * ap
    h = x_ref[...] + _dot(mixed.astype(BF16), wout_ref[...])
    z = _rms(h, g2_ref[...]).astype(BF16)
    gate = _dot(z, wg_ref[...])
    act = (gate * jax.nn.sigmoid(gate) * _dot(z, wu_ref[...])).astype(BF16)
    h = h + _dot(act, wd_ref[...])
    y_ref[...] = _rms(h, gf_ref[...])


def _out_ffn(x2d, ao, sga, cyg, w, *, R):
    n_rows = x2d.shape[0]
    row_map = lambda i: (i, 0)
    consts = [w["wao"], w["wout"], w["g2"], w["wg"], w["wu"], w["wd"], w["gf"]]
    in_specs = [pl.BlockSpec((R, D_MODEL), row_map), pl.BlockSpec((R, D_ATTN), row_map),
                pl.BlockSpec((R, D_MODEL), row_map), pl.BlockSpec((R, D_MODEL), row_map)]
    in_specs += [_const_spec(c.shape) for c in consts]
    return pl.pallas_call(
        _out_ffn_kernel,
        grid=(n_rows // R,),
        in_specs=in_specs,
        out_specs=pl.BlockSpec((R, D_MODEL), row_map),
        out_shape=jax.ShapeDtypeStruct((n_rows, D_MODEL), F32),
        compiler_params=pltpu.CompilerParams(
            dimension_semantics=("arbitrary",), vmem_limit_bytes=VMEM_LIMIT),
        name="out_ffn",
    )(x2d, ao, sga, cyg, *consts)


def _prep_weights(norm_mix_g, w_in, b_f, w_dw, b_dw, ln_g, ln_b, w_conv_pw, w_attn_o, w_out,
                  norm_ffn_g, w_gate, w_up, w_down, final_norm_g):
    o_q = 2 * D_CONV
    o_f = o_q + 3 * D_ATTN
    o_gc = o_f + N_HEADS
    o_ga = o_gc + D_MODEL
    row = lambda a: a.reshape(1, -1).astype(F32)
    return {
        "g1": row(norm_mix_g),
        "wglu": w_in[:, :o_q].astype(BF16),
        "wqkv": w_in[:, o_q:o_f].astype(BF16),
        "wf": jnp.pad(w_in[:, o_f:o_gc], ((0, 0), (0, LANES - N_HEADS))).astype(BF16),
        "bf": jnp.pad(row(b_f), ((0, 0), (0, LANES - N_HEADS))),
        "wgc": w_in[:, o_gc:o_ga].astype(BF16),
        "wga": w_in[:, o_ga:].astype(BF16),
        "wdw": jnp.pad(w_dw.astype(F32), ((0, HIST_PAD - CONV_WIDTH), (0, 0))),
        "bdw": row(b_dw), "lng": row(ln_g), "lnb": row(ln_b),
        "wpw": w_conv_pw.astype(BF16),
        "wao": w_attn_o.astype(BF16), "wout": w_out.astype(BF16), "g2": row(norm_ffn_g),
        "wg": w_gate.astype(BF16), "wu": w_up.astype(BF16), "wd": w_down.astype(BF16),
        "gf": row(final_norm_g),
    }


def kernel(x_prompt, x_sample, cache_k, cache_v, cache_logf, state_conv, norm_mix_g, w_in, b_f,
           w_dw, b_dw, ln_g, ln_b, w_conv_pw, w_attn_o, w_out, norm_ffn_g, w_gate, w_up, w_down,
           final_norm_g):
    B, S, _ = x_prompt.shape
    NB, T, _ = x_sample.shape
    P = cache_k.shape[2]
    w = _prep_weights(norm_mix_g[0], w_in[0], b_f[0], w_dw[0], b_dw[0], ln_g[0], ln_b[0],
                      w_conv_pw[0], w_attn_o[0], w_out[0], norm_ffn_g[0], w_gate[0], w_up[0],
                      w_down[0], final_norm_g)

    xp = x_prompt.reshape(B * S, D_MODEL)
    hist0 = jnp.zeros((B, HIST_PAD, D_CONV), F32)
    q, qx, kx, kb, vb, k5, v5, lf, _, sga, cyg, nh = _inproj(xp, hist0, w, nb=1, T=256, carry=True)
    ao = _attn_prompt(q, qx, kb, kx, vb, batch=B, seq=S, tq=1024, tk=1024, tkd=512, hps=1)
    y_prompt = _out_ffn(xp, ao, sga, cyg, w, R=512).reshape(B, S, D_MODEL)
    k_prompt = k5.reshape(1, B, S, N_HEADS, HEAD_DIM)
    v_prompt = v5.reshape(1, B, S, N_HEADS, HEAD_DIM)
    logf_prompt = jnp.transpose(lf, (0, 2, 1)).reshape(1, B, S, N_HEADS)
    conv_prompt = nh[:, HIST_PAD - HIST:, :].reshape(1, B, HIST, D_CONV)

    xs = x_sample.reshape(NB * T, D_MODEL)
    hist_s = jnp.pad(state_conv[0].astype(F32), ((0, 0), (HIST_PAD - HIST, 0), (0, 0)))
    q, _, _, kb, vb, k5, v5, lf, c, sga, cyg, nh = _inproj(xs, hist_s, w, nb=NB, T=T, carry=False)
    r_new = -jnp.transpose(c.reshape(N_HEADS, NB, T), (1, 0, 2))
    clf_t = jnp.transpose(cache_logf[0].astype(F32), (0, 2, 1)).reshape(NB * N_HEADS, P)
    r_cache = _suffix_sum(clf_t).reshape(NB, N_HEADS, P)
    ao = _attn_sample(q.reshape(NB, T, D_ATTN), jnp.transpose(cache_k[0], (0, 2, 3, 1)),
                      jnp.transpose(cache_v[0], (0, 2, 3, 1)), r_cache,
                      kb.reshape(NB, T, D_ATTN), vb.reshape(NB, T, D_ATTN), r_new, tk=P)
    y_sample = _out_ffn(xs, ao.reshape(NB * T, D_ATTN), sga, cyg, w, R=NB * T)
    y_sample = y_sample.reshape(NB, T, D_MODEL)
    k_sample = k5.reshape(1, NB, T, N_HEADS, HEAD_DIM)
    v_sample = v5.reshape(1, NB, T, N_HEADS, HEAD_DIM)
    logf_sample = jnp.transpose(lf.reshape(N_HEADS, NB, T), (1, 2, 0)).reshape(1, NB, T, N_HEADS)
    conv_sample = nh[:, HIST_PAD - HIST:, :].reshape(1, NB, HIST, D_CONV)

    return (y_prompt, y_sample, k_prompt, v_prompt, logf_prompt, conv_prompt,
            k_sample, v_sample, logf_sample, conv_sample)
```

```python
import functools

import jax
import jax.numpy as jnp
import numpy as np
from jax import lax
from jax.experimental import pallas as pl
from jax.experimental.pallas import tpu as pltpu

D_MODEL = 1024
N_HEADS = 8
HEAD_DIM = 64
D_ATTN = N_HEADS * HEAD_DIM
D_CONV = D_MODEL // 2
CONV_WIDTH = 31
HIST = CONV_WIDTH - 1
D_FF = 2816
RMS_EPS = 1e-6
LN_EPS = 1e-5
NEG = -1e30
SCALE = HEAD_DIM ** -0.5

LANES = 128
HIST_PAD = 32
CONV_ROWS = 64
ROW_BLOCK = 128
VMEM_LIMIT = 56 * 1024 * 1024

F32 = jnp.float32
BF16 = jnp.bfloat16


def _dot(a, b):
    return jnp.dot(a, b, preferred_element_type=F32)


def _dot_nt(a, b):
    return lax.dot_general(a, b, (((1,), (1,)), ((), ())), preferred_element_type=F32)


def _const_spec(shape):
    return pl.BlockSpec(shape, lambda *_: (0,) * len(shape), pipeline_mode=pl.Buffered(1))


def _split3(c):
    hi = c.astype(BF16)
    r1 = c - hi.astype(F32)
    mid = r1.astype(BF16)
    lo = (r1 - mid.astype(F32)).astype(BF16)
    return hi, mid, lo


def _inproj_kernel(x_ref, hist_ref, g_ref, wglu_ref, wqkv_ref, wf_ref, bf_ref, wgc_ref, wga_ref,
                   wdw_ref, bdw_ref, lng_ref, lnb_ref, wpw_ref, pq_ref, pk_ref, cq_ref, ck_ref,
                   q_ref, qx_ref, kx_ref, kb_ref, vb_ref, k5_ref, v5_ref, lf_ref, c_ref, sga_ref,
                   cyg_ref, nh_ref, cpad_ref, act_ref, ccar_ref, sgc_ref, *, nb, T, carry):
    R = nb * T
    RB = min(T, CONV_ROWS)
    t = pl.program_id(1)

    xf = x_ref[...]
    ms = jnp.mean(xf * xf, axis=-1, keepdims=True)
    xn = (xf * lax.rsqrt(ms + RMS_EPS) * g_ref[...]).astype(BF16)

    glu = _dot(xn, wglu_ref[...])
    cu = glu[:, :D_CONV] * jax.nn.sigmoid(glu[:, D_CONV:])
    if carry:
        @pl.when(t == 0)
        def _():
            cpad_ref[:, 0:HIST_PAD, :] = hist_ref[...]
            ccar_ref[...] = jnp.zeros_like(ccar_ref)
    else:
        cpad_ref[:, 0:HIST_PAD, :] = hist_ref[...]
    for b in range(nb):
        cpad_ref[b, HIST_PAD:HIST_PAD + T, :] = cu[b * T:(b + 1) * T]

    def qkv_stage():
        qkv = _dot(xn, wqkv_ref[...])
        q_ref[...] = (qkv[:, :D_ATTN] * SCALE).astype(BF16)
        kf = qkv[:, D_ATTN:2 * D_ATTN]
        vf = qkv[:, 2 * D_ATTN:]
        kb_ref[...] = kf.astype(BF16)
        vb_ref[...] = vf.astype(BF16)
        for h in range(N_HEADS):
            hs = slice(h * HEAD_DIM, (h + 1) * HEAD_DIM)
            k5_ref[pl.ds(h, R, stride=N_HEADS), :] = kf[:, hs]
            v5_ref[pl.ds(h, R, stride=N_HEADS), :] = vf[:, hs]

    def attn_gate_stage():
        sga_ref[...] = jax.nn.sigmoid(_dot(xn, wga_ref[...])).astype(BF16)

    def conv_gate_stage():
        sgc_ref[...] = jax.nn.sigmoid(_dot(xn, wgc_ref[...]))

    def forget_stage():
        z = _dot(xn, wf_ref[...]) + bf_ref[...]
        lf = -(jnp.maximum(-z, 0.0) + jnp.log1p(jnp.exp(-jnp.abs(z))))
        lane = lax.broadcasted_iota(jnp.int32, (R, LANES), 1)
        lf = jnp.where(lane < N_HEADS, lf, 0.0)
        lf_ref[0] = lf.T[:N_HEADS]
        row = lax.broadcasted_iota(jnp.int32, (R, LANES), 0) % T
        c = lf
        d = 1
        while d < T:
            c = c + jnp.where(row >= d, pltpu.roll(c, d, axis=0), 0.0)
            d *= 2
        if carry:
            c = c + ccar_ref[0:1, :]
            ccar_ref[...] = jnp.broadcast_to(c[R - 1:R, :], ccar_ref.shape)
        c_ref[0] = c.T[:N_HEADS]
        hi, mid, lo = _split3(c)
        qx = _dot(hi, pq_ref[0]) + _dot(mid, pq_ref[1]) + _dot(lo, pq_ref[2]) + cq_ref[...]
        kx = _dot(hi, pk_ref[0]) + _dot(mid, pk_ref[1]) + _dot(lo, pk_ref[2]) + ck_ref[...]
        qx_ref[...] = qx.astype(BF16)
        kx_ref[...] = kx.astype(BF16)

    stages = [qkv_stage, attn_gate_stage, conv_gate_stage, forget_stage]
    blocks = [(b, r0) for b in range(nb) for r0 in range(0, T, RB)]
    per_stage = len(blocks) // len(stages)

    off = HIST_PAD - HIST
    win_rows = RB + HIST_PAD
    for n, (b, r0) in enumerate(blocks):
        tiles = []
        for lt in range(D_CONV // LANES):
            ls = slice(lt * LANES, (lt + 1) * LANES)
            win = cpad_ref[b, r0:r0 + win_rows, ls]
            acc = jnp.broadcast_to(bdw_ref[:, ls], (RB, LANES))
            for res in range(8):
                sh = win if res == 0 else pltpu.roll(win, win_rows - res, axis=0)
                for a in range(HIST_PAD // 8 + 1):
                    w = 8 * a + res - off
                    if 0 <= w < CONV_WIDTH:
                        acc = acc + sh[8 * a:8 * a + RB] * wdw_ref[w:w + 1, ls]
            tiles.append(acc)
        acc = jnp.concatenate(tiles, axis=1)
        mu = jnp.mean(acc, axis=-1, keepdims=True)
        dv = acc - mu
        var = jnp.mean(dv * dv, axis=-1, keepdims=True)
        y = dv * lax.rsqrt(var + LN_EPS) * lng_ref[...] + lnb_ref[...]
        act_ref[b * T + r0:b * T + r0 + RB, :] = (y * jax.nn.sigmoid(y)).astype(BF16)
        if n % per_stage == 0 and stages:
            stages.pop(0)()
    assert not stages

    nh_ref[...] = cpad_ref[:, T:T + HIST_PAD, :]
    if carry:
        cpad_ref[:, 0:HIST_PAD, :] = cpad_ref[:, T:T + HIST_PAD, :]

    cy = _dot(act_ref[...], wpw_ref[...])
    cyg_ref[...] = (sgc_ref[...] * cy).astype(BF16)


def _extra_lane(h):
    pair, odd = divmod(h, 2)
    return pair * LANES + (0 if odd else HEAD_DIM)


def _placement_constants():
    pq = np.zeros((3, LANES, D_ATTN), np.float32)
    pk = np.zeros((3, LANES, D_ATTN), np.float32)
    cq = np.zeros((1, D_ATTN), np.float32)
    ck = np.zeros((1, D_ATTN), np.float32)
    for h in range(N_HEADS):
        p = _extra_lane(h)
        for part in range(3):
            pq[part, h, p + part] = 1.0
            pk[part, h, p + 3 + part] = -1.0
            cq[0, p + 3 + part] = 1.0
            ck[0, p + part] = 1.0
    return pq, pk, cq, ck


def _inproj(x2d, hist, w, *, nb, T, carry):
    n_rows = x2d.shape[0]
    R = nb * T
    n_seg = hist.shape[0]
    n_outer = n_seg // nb
    n_t = n_rows // (R * n_outer)
    grid = (n_outer, n_t)
    row_map = lambda b, t: (b * n_t + t, 0)
    pq, pk, cq, ck = _placement_constants()

    def rows(width, dtype):
        return jax.ShapeDtypeStruct((n_rows, width), dtype), pl.BlockSpec((R, width), row_map)

    def head_rows():
        return (jax.ShapeDtypeStruct((n_rows * N_HEADS, HEAD_DIM), F32),
                pl.BlockSpec((R * N_HEADS, HEAD_DIM), row_map))

    def head_major():
        return (jax.ShapeDtypeStruct((n_outer, N_HEADS, n_t * R), F32),
                pl.BlockSpec((1, N_HEADS, R), lambda b, t: (b, 0, t)))

    outs = [rows(D_ATTN, BF16), rows(D_ATTN, BF16), rows(D_ATTN, BF16), rows(D_ATTN, BF16),
            rows(D_ATTN, BF16), head_rows(), head_rows(), head_major(), head_major(),
            rows(D_MODEL, BF16), rows(D_MODEL, BF16)]
    out_shape = [o[0] for o in outs] + [jax.ShapeDtypeStruct((n_seg, HIST_PAD, D_CONV), F32)]
    out_specs = [o[1] for o in outs] + [pl.BlockSpec((nb, HIST_PAD, D_CONV), lambda b, t: (b, 0, 0))]

    consts = [w["g1"], w["wglu"], w["wqkv"], w["wf"], w["bf"], w["wgc"], w["wga"], w["wdw"],
              w["bdw"], w["lng"], w["lnb"], w["wpw"], jnp.asarray(pq, BF16), jnp.asarray(pk, BF16),
              jnp.asarray(cq), jnp.asarray(ck)]
    in_specs = [pl.BlockSpec((R, D_MODEL), row_map),
                pl.BlockSpec((nb, HIST_PAD, D_CONV), lambda b, t: (b, 0, 0))]
    in_specs += [_const_spec(c.shape) for c in consts]

    return pl.pallas_call(
        functools.partial(_inproj_kernel, nb=nb, T=T, carry=carry),
        grid=grid,
        in_specs=in_specs,
        out_specs=out_specs,
        out_shape=out_shape,
        scratch_shapes=[pltpu.VMEM((nb, T + HIST_PAD, D_CONV), F32),
                        pltpu.VMEM((R, D_CONV), BF16),
                        pltpu.VMEM((8, LANES), F32),
                        pltpu.VMEM((R, D_MODEL), F32)],
        compiler_params=pltpu.CompilerParams(
            dimension_semantics=("arbitrary", "arbitrary"), vmem_limit_bytes=VMEM_LIMIT),
        name="inproj_conv",
    )(x2d, hist, *consts)


def _attn_prompt_kernel(q_ref, qx_ref, k_ref, kx_ref, v_ref, o_ref, ka_ref, va_ref, *, tq, tk, tkd, hps):
    i = pl.program_id(2)
    nh = 2 * hps
    lane = lax.broadcasted_iota(jnp.int32, (tq, LANES), 1)
    low = lane < HEAD_DIM
    row0 = pl.multiple_of(i * tq, tq)
    one_hi = (lane == HEAD_DIM).astype(BF16)
    one_lo = (lane == 0).astype(BF16)

    qa = []
    for hp in range(hps):
        ps = slice(hp * LANES, (hp + 1) * LANES)
        ev = slice(2 * hp * LANES, (2 * hp + 1) * LANES)
        od = slice((2 * hp + 1) * LANES, (2 * hp + 2) * LANES)
        q, qx, k, kx, v = (r[0, :, ps] for r in (q_ref, qx_ref, k_ref, kx_ref, v_ref))
        ka_ref[pl.ds(row0, tq), ev] = jnp.where(low, k, kx)
        ka_ref[pl.ds(row0, tq), od] = jnp.where(low, kx, k)
        va_ref[pl.ds(row0, tq), ev] = jnp.where(low, v, one_hi)
        va_ref[pl.ds(row0, tq), od] = jnp.where(low, one_lo, v)
        qa += [jnp.where(low, q, qx), jnp.where(low, qx, q)]

    def step(col0, width, carry, r_lo=None):
        masked = r_lo is not None
        r_lo = r_lo or 0
        out = []
        for e in range(nh):
            m, acc = carry[e]
            kj = ka_ref[pl.ds(col0, width), e * LANES:(e + 1) * LANES]
            vj = va_ref[pl.ds(col0, width), e * LANES:(e + 1) * LANES]
            s = _dot_nt(qa[e][r_lo:], kj)
            if masked:
                r_id = lax.broadcasted_iota(jnp.int32, (tq - r_lo, width), 0)
                c_id = lax.broadcasted_iota(jnp.int32, (tq - r_lo, width), 1)
                s = jnp.where(c_id <= r_id, s, NEG)
            m_new = jnp.maximum(m[r_lo:], jnp.max(s, axis=-1, keepdims=True))
            alpha = jnp.exp(m[r_lo:] - m_new)
            p = jnp.exp(s - m_new).astype(BF16)
            acc_new = alpha * acc[r_lo:] + _dot(p, vj)
            if r_lo:
                m_new = jnp.concatenate([m[:r_lo], m_new], axis=0)
                acc_new = jnp.concatenate([acc[:r_lo], acc_new], axis=0)
            out.append((m_new, acc_new))
        return tuple(out)

    init = tuple((jnp.full((tq, 1), NEG, F32), jnp.zeros((tq, LANES), F32)) for _ in range(nh))
    carry = lax.fori_loop(0, i * (tq // tk),
                          lambda j, c: step(pl.multiple_of(j * tk, tk), tk, c), init)
    for d in range(tq // tkd):
        carry = step(pl.multiple_of(row0 + d * tkd, tkd), tkd, carry, r_lo=d * tkd)
    for hp in range(hps):
        acc0, acc1 = carry[2 * hp][1], carry[2 * hp + 1][1]
        o0 = acc0 / acc0[:, HEAD_DIM:HEAD_DIM + 1]
        o1 = acc1 / acc1[:, 0:1]
        o_ref[0, :, hp * LANES:(hp + 1) * LANES] = jnp.where(low, o0, o1).astype(BF16)


def _attn_prompt(q, qx, k, kx, v, *, batch, seq, tq, tk, tkd, hps):
    shp = (batch, seq, D_ATTN)
    q, qx, k, kx, v = (a.reshape(shp) for a in (q, qx, k, kx, v))
    spec = pl.BlockSpec((1, tq, hps * LANES), lambda b, g, i: (b, i, g))
    return pl.pallas_call(
        functools.partial(_attn_prompt_kernel, tq=tq, tk=tk, tkd=tkd, hps=hps),
        grid=(batch, N_HEADS // (2 * hps), seq // tq),
        in_specs=[spec] * 5,
        out_specs=spec,
        out_shape=jax.ShapeDtypeStruct(shp, BF16),
        scratch_shapes=[pltpu.VMEM((seq, 2 * hps * LANES), BF16),
                        pltpu.VMEM((seq, 2 * hps * LANES), BF16)],
        compiler_params=pltpu.CompilerParams(
            dimension_semantics=("arbitrary", "arbitrary", "arbitrary"),
            vmem_limit_bytes=VMEM_LIMIT),
        name="attn_prompt",
    )(q, qx, k, kx, v).reshape(batch * seq, D_ATTN)


def _suffix_sum_kernel(x_ref, o_ref):
    x = x_ref[...]
    n = x.shape[1]
    lane = lax.broadcasted_iota(jnp.int32, x.shape, 1)
    s = x
    d = 1
    while d < n:
        s = s + jnp.where(lane < n - d, pltpu.roll(s, n - d, axis=1), 0.0)
        d *= 2
    o_ref[...] = s - x


def _suffix_sum(x):
    return pl.pallas_call(
        _suffix_sum_kernel,
        out_shape=jax.ShapeDtypeStruct(x.shape, F32),
        compiler_params=pltpu.CompilerParams(vmem_limit_bytes=VMEM_LIMIT),
        name="cache_decay",
    )(x)


def _attn_sample_kernel(q_ref, ck_ref, cv_ref, r_ref, kn_ref, vn_ref, rn_ref, o_ref,
                        m_ref, l_ref, acc_ref, *, nq):
    j = pl.program_id(1)
    heads = range(N_HEADS)
    rows = [slice(h * nq, (h + 1) * nq) for h in heads]
    lanes = [slice(h * HEAD_DIM, (h + 1) * HEAD_DIM) for h in heads]
    q = q_ref[0]

    def stack(parts):
        return jnp.concatenate(parts, axis=0)

    def update(scores, decay, pv, mask=None):
        n = decay.shape[1]
        s = stack(scores) + stack([jnp.broadcast_to(decay[h:h + 1, :], (nq, n)) for h in heads])
        if mask is not None:
            s = jnp.where(mask, s, NEG)
        m_old = m_ref[:, 0:1]
        m_new = jnp.maximum(m_old, jnp.max(s, axis=-1, keepdims=True))
        alpha = jnp.exp(m_old - m_new)
        p = jnp.exp(s - m_new)
        l_new = alpha * l_ref[:, 0:1] + jnp.sum(p, axis=-1, keepdims=True)
        p = p.astype(BF16)
        acc_ref[...] = alpha * acc_ref[...] + stack([pv(h, p[rows[h]]) for h in heads])
        m_ref[...] = jnp.broadcast_to(m_new, m_ref.shape)
        l_ref[...] = jnp.broadcast_to(l_new, l_ref.shape)

    @pl.when(j == 0)
    def _():
        m_ref[...] = jnp.full(m_ref.shape, NEG, F32)
        l_ref[...] = jnp.zeros_like(l_ref)
        acc_ref[...] = jnp.zeros_like(acc_ref)
        q_id = lax.broadcasted_iota(jnp.int32, (N_HEADS * nq, nq), 0) % nq
        k_id = lax.broadcasted_iota(jnp.int32, (N_HEADS * nq, nq), 1)
        kn = kn_ref[0]
        vn = vn_ref[0]
        update([_dot_nt(q[:, lanes[h]], kn[:, lanes[h]]) for h in heads], rn_ref[0],
               lambda h, p: _dot(p, vn[:, lanes[h]]), mask=k_id <= q_id)

    update([_dot(q[:, lanes[h]], ck_ref[0, h].astype(BF16)) for h in heads], r_ref[0],
           lambda h, p: _dot_nt(p, cv_ref[0, h].astype(BF16)))

    @pl.when(j == pl.num_programs(1) - 1)
    def _():
        o = (acc_ref[...] / l_ref[:, 0:1]).astype(BF16)
        for h in heads:
            o_ref[0, :, lanes[h]] = o[rows[h]]


def _attn_sample(qh, cache_kt, cache_vt, r_cache, k_new, v_new, r_new, *, tk):
    nb, _, _, past = cache_kt.shape
    nq = k_new.shape[1]
    cache_spec = pl.BlockSpec((1, N_HEADS, HEAD_DIM, tk), lambda b, j: (b, 0, 0, j))
    return pl.pallas_call(
        functools.partial(_attn_sample_kernel, nq=nq),
        grid=(nb, past // tk),
        in_specs=[pl.BlockSpec((1, nq, D_ATTN), lambda b, j: (b, 0, 0)),
                  cache_spec, cache_spec,
                  pl.BlockSpec((1, N_HEADS, tk), lambda b, j: (b, 0, j)),
                  pl.BlockSpec((1, nq, D_ATTN), lambda b, j: (b, 0, 0)),
                  pl.BlockSpec((1, nq, D_ATTN), lambda b, j: (b, 0, 0)),
                  pl.BlockSpec((1, N_HEADS, nq), lambda b, j: (b, 0, 0))],
        out_specs=pl.BlockSpec((1, nq, D_ATTN), lambda b, j: (b, 0, 0)),
        out_shape=jax.ShapeDtypeStruct((nb, nq, D_ATTN), BF16),
        scratch_shapes=[pltpu.VMEM((N_HEADS * nq, LANES), F32),
                        pltpu.VMEM((N_HEADS * nq, LANES), F32),
                        pltpu.VMEM((N_HEADS * nq, HEAD_DIM), F32)],
        compiler_params=pltpu.CompilerParams(
            dimension_semantics=("arbitrary", "arbitrary"), vmem_limit_bytes=VMEM_LIMIT),
        name="attn_sample",
    )(qh, cache_kt, cache_vt, r_cache, k_new, v_new, r_new)


def _rms(x, g):
    return x * lax.rsqrt(jnp.mean(x * x, axis=-1, keepdims=True) + RMS_EPS) * g


def _out_ffn_rows(x_ref, ao_ref, sga_ref, cyg_ref, wao_ref, wout_ref, g2_ref, wg_ref, wu_ref,
                  wd_ref, gf_ref, y_ref):
    ap = _dot(ao_ref[...], wao_ref[...])
    mixed = cyg_ref[...].astype(F32) + sga_ref[...].astype(F32) * ap
    h = x_ref[...] + _dot(mixed.astype(BF16), wout_ref[...])
    z = _rms(h, g2_ref[...]).astype(BF16)
    gate = _dot(z, wg_ref[...])
    act = (gate * jax.nn.sigmoid(gate) * _dot(z, wu_ref[...])).astype(BF16)
    h = h + _dot(act, wd_ref[...])
    y_ref[...] = _rms(h, gf_ref[...])


def _out_ffn_kernel(*refs, n_prompt):
    prompt_in, sample_in, consts = refs[0:4], refs[4:8], refs[8:15]
    y_prompt_ref, y_sample_ref = refs[15:]
    i = pl.program_id(0)

    @pl.when(i < n_prompt)
    def _():
        _out_ffn_rows(*prompt_in, *consts, y_prompt_ref)

    @pl.when(i == n_prompt)
    def _():
        _out_ffn_rows(*sample_in, *consts, y_sample_ref)


def _out_ffn(prompt, sample, w, *, R):
    n_prompt = prompt[0].shape[0] // R
    assert sample[0].shape[0] == R
    widths = (D_MODEL, D_ATTN, D_MODEL, D_MODEL)
    prompt_map = lambda i: (jnp.minimum(i, n_prompt - 1), 0)
    sample_map = lambda i: (0, 0)
    consts = [w["wao"], w["wout"], w["g2"], w["wg"], w["wu"], w["wd"], w["gf"]]
    in_specs = [pl.BlockSpec((R, n), prompt_map) for n in widths]
    in_specs += [pl.BlockSpec((R, n), sample_map, pipeline_mode=pl.Buffered(1)) for n in widths]
    in_specs += [_const_spec(c.shape) for c in consts]
    return pl.pallas_call(
        functools.partial(_out_ffn_kernel, n_prompt=n_prompt),
        grid=(n_prompt + 1,),
        in_specs=in_specs,
        out_specs=[pl.BlockSpec((R, D_MODEL), prompt_map), pl.BlockSpec((R, D_MODEL), sample_map)],
        out_shape=[jax.ShapeDtypeStruct((prompt[0].shape[0], D_MODEL), F32),
                   jax.ShapeDtypeStruct((R, D_MODEL), F32)],
        compiler_params=pltpu.CompilerParams(
            dimension_semantics=("arbitrary",), vmem_limit_bytes=VMEM_LIMIT),
        name="out_ffn",
    )(*prompt, *sample, *consts)


def _prep_weights(norm_mix_g, w_in, b_f, w_dw, b_dw, ln_g, ln_b, w_conv_pw, w_attn_o, w_out,
                  norm_ffn_g, w_gate, w_up, w_down, final_norm_g):
    o_q = 2 * D_CONV
    o_f = o_q + 3 * D_ATTN
    o_gc = o_f + N_HEADS
    o_ga = o_gc + D_MODEL
    row = lambda a: a.reshape(1, -1).astype(F32)
    return {
        "g1": row(norm_mix_g),
        "wglu": w_in[:, :o_q].astype(BF16),
        "wqkv": w_in[:, o_q:o_f].astype(BF16),
        "wf": jnp.pad(w_in[:, o_f:o_gc], ((0, 0), (0, LANES - N_HEADS))).astype(BF16),
        "bf": jnp.pad(row(b_f), ((0, 0), (0, LANES - N_HEADS))),
        "wgc": w_in[:, o_gc:o_ga].astype(BF16),
        "wga": w_in[:, o_ga:].astype(BF16),
        "wdw": jnp.pad(w_dw.astype(F32), ((0, HIST_PAD - CONV_WIDTH), (0, 0))),
        "bdw": row(b_dw), "lng": row(ln_g), "lnb": row(ln_b),
        "wpw": w_conv_pw.astype(BF16),
        "wao": w_attn_o.astype(BF16), "wout": w_out.astype(BF16), "g2": row(norm_ffn_g),
        "wg": w_gate.astype(BF16), "wu": w_up.astype(BF16), "wd": w_down.astype(BF16),
        "gf": row(final_norm_g),
    }


def kernel(x_prompt, x_sample, cache_k, cache_v, cache_logf, state_conv, norm_mix_g, w_in, b_f,
           w_dw, b_dw, ln_g, ln_b, w_conv_pw, w_attn_o, w_out, norm_ffn_g, w_gate, w_up, w_down,
           final_norm_g):
    B, S, _ = x_prompt.shape
    NB, T, _ = x_sample.shape
    P = cache_k.shape[2]
    w = _prep_weights(norm_mix_g[0], w_in[0], b_f[0], w_dw[0], b_dw[0], ln_g[0], ln_b[0],
                      w_conv_pw[0], w_attn_o[0], w_out[0], norm_ffn_g[0], w_gate[0], w_up[0],
                      w_down[0], final_norm_g)

    xp = x_prompt.reshape(B * S, D_MODEL)
    hist0 = jnp.zeros((B, HIST_PAD, D_CONV), F32)
    q, qx, kx, kb, vb, k5, v5, lf, _, sga, cyg, nh = _inproj(xp, hist0, w, nb=1, T=512, carry=True)
    ao = _attn_prompt(q, qx, kb, kx, vb, batch=B, seq=S, tq=1024, tk=1024, tkd=512, hps=1)
    prompt_rows = (xp, ao, sga, cyg)
    k_prompt = k5.reshape(1, B, S, N_HEADS, HEAD_DIM)
    v_prompt = v5.reshape(1, B, S, N_HEADS, HEAD_DIM)
    logf_prompt = jnp.transpose(lf, (0, 2, 1)).reshape(1, B, S, N_HEADS)
    conv_prompt = nh[:, HIST_PAD - HIST:, :].reshape(1, B, HIST, D_CONV)

    xs = x_sample.reshape(NB * T, D_MODEL)
    hist_s = jnp.pad(state_conv[0].astype(F32), ((0, 0), (HIST_PAD - HIST, 0), (0, 0)))
    q, _, _, kb, vb, k5, v5, lf, c, sga, cyg, nh = _inproj(xs, hist_s, w, nb=NB, T=T, carry=False)
    r_new = -jnp.transpose(c.reshape(N_HEADS, NB, T), (1, 0, 2))
    clf_t = jnp.transpose(cache_logf[0].astype(F32), (0, 2, 1)).reshape(NB * N_HEADS, P)
    r_cache = _suffix_sum(clf_t).reshape(NB, N_HEADS, P)
    ao = _attn_sample(q.reshape(NB, T, D_ATTN), jnp.transpose(cache_k[0], (0, 2, 3, 1)),
                      jnp.transpose(cache_v[0], (0, 2, 3, 1)), r_cache,
                      kb.reshape(NB, T, D_ATTN), vb.reshape(NB, T, D_ATTN), r_new, tk=P)
    y_prompt, y_sample = _out_ffn(prompt_rows, (xs, ao.reshape(NB * T, D_ATTN), sga, cyg), w,
                                  R=NB * T)
    y_prompt = y_prompt.reshape(B, S, D_MODEL)
    y_sample = y_sample.reshape(NB, T, D_MODEL)
    k_sample = k5.reshape(1, NB, T, N_HEADS, HEAD_DIM)
    v_sample = v5.reshape(1, NB, T, N_HEADS, HEAD_DIM)
    logf_sample = jnp.transpose(lf.reshape(N_HEADS, NB, T), (1, 2, 0)).reshape(1, NB, T, N_HEADS)
    conv_sample = nh[:, HIST_PAD - HIST:, :].reshape(1, NB, HIST, D_CONV)

    return (y_prompt, y_sample, k_prompt, v_prompt, logf_prompt, conv_prompt,
            k_sample, v_sample, logf_sample, conv_sample)
```

```python
import functools

import jax
import jax.numpy as jnp
import numpy as np
from jax import lax
from jax.experimental import pallas as pl
from jax.experimental.pallas import tpu as pltpu

D_MODEL = 1024
N_HEADS = 8
HEAD_DIM = 64
D_ATTN = N_HEADS * HEAD_DIM
D_CONV = D_MODEL // 2
CONV_WIDTH = 31
HIST = CONV_WIDTH - 1
D_FF = 2816
RMS_EPS = 1e-6
LN_EPS = 1e-5
NEG = -1e30
SCALE = HEAD_DIM ** -0.5

LANES = 128
HIST_PAD = 32
CONV_ROWS = 64
ROW_BLOCK = 128
VMEM_LIMIT = 56 * 1024 * 1024

F32 = jnp.float32
BF16 = jnp.bfloat16


def _dot(a, b):
    return jnp.dot(a, b, preferred_element_type=F32)


def _dot_nt(a, b):
    return lax.dot_general(a, b, (((1,), (1,)), ((), ())), preferred_element_type=F32)


def _const_spec(shape):
    return pl.BlockSpec(shape, lambda *_: (0,) * len(shape), pipeline_mode=pl.Buffered(1))


def _split3(c):
    hi = c.astype(BF16)
    r1 = c - hi.astype(F32)
    mid = r1.astype(BF16)
    lo = (r1 - mid.astype(F32)).astype(BF16)
    return hi, mid, lo


def _inproj_kernel(x_ref, hist_ref, g_ref, wglu_ref, wqkv_ref, wf_ref, bf_ref, wgc_ref, wga_ref,
                   wdw_ref, bdw_ref, lng_ref, lnb_ref, wpw_ref, pq_ref, pk_ref, cq_ref, ck_ref,
                   q_ref, qx_ref, kx_ref, kb_ref, vb_ref, k5_ref, v5_ref, lf_ref, c_ref, sga_ref,
                   cyg_ref, nh_ref, cpad_ref, act_ref, ccar_ref, sgc_ref, *, nb, T, carry):
    R = nb * T
    RB = min(T, CONV_ROWS)
    t = pl.program_id(1)

    xf = x_ref[...]
    ms = jnp.mean(xf * xf, axis=-1, keepdims=True)
    xn = (xf * lax.rsqrt(ms + RMS_EPS) * g_ref[...]).astype(BF16)

    glu = _dot(xn, wglu_ref[...])
    cu = glu[:, :D_CONV] * jax.nn.sigmoid(glu[:, D_CONV:])
    if carry:
        @pl.when(t == 0)
        def _():
            cpad_ref[:, 0:HIST_PAD, :] = hist_ref[...]
            ccar_ref[...] = jnp.zeros_like(ccar_ref)
    else:
        cpad_ref[:, 0:HIST_PAD, :] = hist_ref[...]
    for b in range(nb):
        cpad_ref[b, HIST_PAD:HIST_PAD + T, :] = cu[b * T:(b + 1) * T]

    def qkv_stage():
        qkv = _dot(xn, wqkv_ref[...])
        q_ref[...] = (qkv[:, :D_ATTN] * SCALE).astype(BF16)
        kf = qkv[:, D_ATTN:2 * D_ATTN]
        vf = qkv[:, 2 * D_ATTN:]
        kb_ref[...] = kf.astype(BF16)
        vb_ref[...] = vf.astype(BF16)
        for h in range(N_HEADS):
            hs = slice(h * HEAD_DIM, (h + 1) * HEAD_DIM)
            k5_ref[pl.ds(h, R, stride=N_HEADS), :] = kf[:, hs]
            v5_ref[pl.ds(h, R, stride=N_HEADS), :] = vf[:, hs]

    def attn_gate_stage():
        sga_ref[...] = jax.nn.sigmoid(_dot(xn, wga_ref[...])).astype(BF16)

    def conv_gate_stage():
        sgc_ref[...] = jax.nn.sigmoid(_dot(xn, wgc_ref[...]))

    def forget_stage():
        z = _dot(xn, wf_ref[...]) + bf_ref[...]
        lf = -(jnp.maximum(-z, 0.0) + jnp.log1p(jnp.exp(-jnp.abs(z))))
        lane = lax.broadcasted_iota(jnp.int32, (R, LANES), 1)
        lf = jnp.where(lane < N_HEADS, lf, 0.0)
        lf_ref[0] = lf.T[:N_HEADS]
        row = lax.broadcasted_iota(jnp.int32, (R, LANES), 0) % T
        c = lf
        d = 1
        while d < T:
            c = c + jnp.where(row >= d, pltpu.roll(c, d, axis=0), 0.0)
            d *= 2
        if carry:
            c = c + ccar_ref[0:1, :]
            ccar_ref[...] = jnp.broadcast_to(c[R - 1:R, :], ccar_ref.shape)
        c_ref[0] = c.T[:N_HEADS]
        hi, mid, lo = _split3(c)
        qx = _dot(hi, pq_ref[0]) + _dot(mid, pq_ref[1]) + _dot(lo, pq_ref[2]) + cq_ref[...]
        kx = _dot(hi, pk_ref[0]) + _dot(mid, pk_ref[1]) + _dot(lo, pk_ref[2]) + ck_ref[...]
        qx_ref[...] = qx.astype(BF16)
        kx_ref[...] = kx.astype(BF16)

    stages = [qkv_stage, attn_gate_stage, conv_gate_stage, forget_stage]
    blocks = [(b, r0) for b in range(nb) for r0 in range(0, T, RB)]
    per_stage = len(blocks) // len(stages)

    off = HIST_PAD - HIST
    win_rows = RB + HIST_PAD
    for n, (b, r0) in enumerate(blocks):
        tiles = []
        for lt in range(D_CONV // LANES):
            ls = slice(lt * LANES, (lt + 1) * LANES)
            win = cpad_ref[b, r0:r0 + win_rows, ls]
            acc = jnp.broadcast_to(bdw_ref[:, ls], (RB, LANES))
            for res in range(8):
                sh = win if res == 0 else pltpu.roll(win, win_rows - res, axis=0)
                for a in range(HIST_PAD // 8 + 1):
                    w = 8 * a + res - off
                    if 0 <= w < CONV_WIDTH:
                        acc = acc + sh[8 * a:8 * a + RB] * wdw_ref[w:w + 1, ls]
            tiles.append(acc)
        acc = jnp.concatenate(tiles, axis=1)
        mu = jnp.mean(acc, axis=-1, keepdims=True)
        dv = acc - mu
        var = jnp.mean(dv * dv, axis=-1, keepdims=True)
        y = dv * lax.rsqrt(var + LN_EPS) * lng_ref[...] + lnb_ref[...]
        act_ref[b * T + r0:b * T + r0 + RB, :] = (y * jax.nn.sigmoid(y)).astype(BF16)
        if n % per_stage == 0 and stages:
            stages.pop(0)()
    assert not stages

    nh_ref[...] = cpad_ref[:, T:T + HIST_PAD, :]
    if carry:
        cpad_ref[:, 0:HIST_PAD, :] = cpad_ref[:, T:T + HIST_PAD, :]

    cy = _dot(act_ref[...], wpw_ref[...])
    cyg_ref[...] = (sgc_ref[...] * cy).astype(BF16)


def _extra_lane(h):
    pair, odd = divmod(h, 2)
    return pair * LANES + (0 if odd else HEAD_DIM)


def _placement_constants():
    pq = np.zeros((3, LANES, D_ATTN), np.float32)
    pk = np.zeros((3, LANES, D_ATTN), np.float32)
    cq = np.zeros((1, D_ATTN), np.float32)
    ck = np.zeros((1, D_ATTN), np.float32)
    for h in range(N_HEADS):
        p = _extra_lane(h)
        for part in range(3):
            pq[part, h, p + part] = 1.0
            pk[part, h, p + 3 + part] = -1.0
            cq[0, p + 3 + part] = 1.0
            ck[0, p + part] = 1.0
    return pq, pk, cq, ck


def _inproj(x2d, hist, w, *, nb, T, carry):
    n_rows = x2d.shape[0]
    R = nb * T
    n_seg = hist.shape[0]
    n_outer = n_seg // nb
    n_t = n_rows // (R * n_outer)
    grid = (n_outer, n_t)
    row_map = lambda b, t: (b * n_t + t, 0)
    pq, pk, cq, ck = _placement_constants()

    def rows(width, dtype):
        return jax.ShapeDtypeStruct((n_rows, width), dtype), pl.BlockSpec((R, width), row_map)

    def head_rows():
        return (jax.ShapeDtypeStruct((n_rows * N_HEADS, HEAD_DIM), F32),
                pl.BlockSpec((R * N_HEADS, HEAD_DIM), row_map))

    def head_major():
        return (jax.ShapeDtypeStruct((n_outer, N_HEADS, n_t * R), F32),
                pl.BlockSpec((1, N_HEADS, R), lambda b, t: (b, 0, t)))

    outs = [rows(D_ATTN, BF16), rows(D_ATTN, BF16), rows(D_ATTN, BF16), rows(D_ATTN, BF16),
            rows(D_ATTN, BF16), head_rows(), head_rows(), head_major(), head_major(),
            rows(D_MODEL, BF16), rows(D_MODEL, BF16)]
    out_shape = [o[0] for o in outs] + [jax.ShapeDtypeStruct((n_seg, HIST_PAD, D_CONV), F32)]
    out_specs = [o[1] for o in outs] + [pl.BlockSpec((nb, HIST_PAD, D_CONV), lambda b, t: (b, 0, 0))]

    consts = [w["g1"], w["wglu"], w["wqkv"], w["wf"], w["bf"], w["wgc"], w["wga"], w["wdw"],
              w["bdw"], w["lng"], w["lnb"], w["wpw"], jnp.asarray(pq, BF16), jnp.asarray(pk, BF16),
              jnp.asarray(cq), jnp.asarray(ck)]
    in_specs = [pl.BlockSpec((R, D_MODEL), row_map),
                pl.BlockSpec((nb, HIST_PAD, D_CONV), lambda b, t: (b, 0, 0))]
    in_specs += [_const_spec(c.shape) for c in consts]

    return pl.pallas_call(
        functools.partial(_inproj_kernel, nb=nb, T=T, carry=carry),
        grid=grid,
        in_specs=in_specs,
        out_specs=out_specs,
        out_shape=out_shape,
        scratch_shapes=[pltpu.VMEM((nb, T + HIST_PAD, D_CONV), F32),
                        pltpu.VMEM((R, D_CONV), BF16),
                        pltpu.VMEM((8, LANES), F32),
                        pltpu.VMEM((R, D_MODEL), F32)],
        compiler_params=pltpu.CompilerParams(
            dimension_semantics=("arbitrary", "arbitrary"), vmem_limit_bytes=VMEM_LIMIT),
        name="inproj_conv",
    )(x2d, hist, *consts)


def _attn_prompt_kernel(q_ref, qx_ref, k_ref, kx_ref, v_ref, o_ref, ka_ref, va_ref, *, tq, tk, tkd, hps):
    i = pl.program_id(2)
    nh = 2 * hps
    lane = lax.broadcasted_iota(jnp.int32, (tq, LANES), 1)
    low = lane < HEAD_DIM
    row0 = pl.multiple_of(i * tq, tq)
    one_hi = (lane == HEAD_DIM).astype(BF16)
    one_lo = (lane == 0).astype(BF16)

    qa = []
    for hp in range(hps):
        ps = slice(hp * LANES, (hp + 1) * LANES)
        ev = slice(2 * hp * LANES, (2 * hp + 1) * LANES)
        od = slice((2 * hp + 1) * LANES, (2 * hp + 2) * LANES)
        q, qx, k, kx, v = (r[0, :, ps] for r in (q_ref, qx_ref, k_ref, kx_ref, v_ref))
        ka_ref[pl.ds(row0, tq), ev] = jnp.where(low, k, kx)
        ka_ref[pl.ds(row0, tq), od] = jnp.where(low, kx, k)
        va_ref[pl.ds(row0, tq), ev] = jnp.where(low, v, one_hi)
        va_ref[pl.ds(row0, tq), od] = jnp.where(low, one_lo, v)
        qa += [jnp.where(low, q, qx), jnp.where(low, qx, q)]

    def step(col0, width, carry, r_lo=None):
        masked = r_lo is not None
        r_lo = r_lo or 0
        out = []
        for e in range(nh):
            m, acc = carry[e]
            kj = ka_ref[pl.ds(col0, width), e * LANES:(e + 1) * LANES]
            vj = va_ref[pl.ds(col0, width), e * LANES:(e + 1) * LANES]
            s = _dot_nt(qa[e][r_lo:], kj)
            if masked:
                r_id = lax.broadcasted_iota(jnp.int32, (tq - r_lo, width), 0)
                c_id = lax.broadcasted_iota(jnp.int32, (tq - r_lo, width), 1)
                s = jnp.where(c_id <= r_id, s, NEG)
            m_new = jnp.maximum(m[r_lo:], jnp.max(s, axis=-1, keepdims=True))
            alpha = jnp.exp(m[r_lo:] - m_new)
            p = jnp.exp(s - m_new).astype(BF16)
            acc_new = alpha * acc[r_lo:] + _dot(p, vj)
            if r_lo:
                m_new = jnp.concatenate([m[:r_lo], m_new], axis=0)
                acc_new = jnp.concatenate([acc[:r_lo], acc_new], axis=0)
            out.append((m_new, acc_new))
        return tuple(out)

    init = tuple((jnp.full((tq, 1), NEG, F32), jnp.zeros((tq, LANES), F32)) for _ in range(nh))
    carry = lax.fori_loop(0, i * (tq // tk),
                          lambda j, c: step(pl.multiple_of(j * tk, tk), tk, c), init)
    for d in range(tq // tkd):
        carry = step(pl.multiple_of(row0 + d * tkd, tkd), tkd, carry, r_lo=d * tkd)
    for hp in range(hps):
        acc0, acc1 = carry[2 * hp][1], carry[2 * hp + 1][1]
        o0 = acc0 / acc0[:, HEAD_DIM:HEAD_DIM + 1]
        o1 = acc1 / acc1[:, 0:1]
        o_ref[0, :, hp * LANES:(hp + 1) * LANES] = jnp.where(low, o0, o1).astype(BF16)


def _attn_prompt(q, qx, k, kx, v, *, batch, seq, tq, tk, tkd, hps):
    shp = (batch, seq, D_ATTN)
    q, qx, k, kx, v = (a.reshape(shp) for a in (q, qx, k, kx, v))
    spec = pl.BlockSpec((1, tq, hps * LANES), lambda b, g, i: (b, i, g))
    return pl.pallas_call(
        functools.partial(_attn_prompt_kernel, tq=tq, tk=tk, tkd=tkd, hps=hps),
        grid=(batch, N_HEADS // (2 * hps), seq // tq),
        in_specs=[spec] * 5,
        out_specs=spec,
        out_shape=jax.ShapeDtypeStruct(shp, BF16),
        scratch_shapes=[pltpu.VMEM((seq, 2 * hps * LANES), BF16),
                        pltpu.VMEM((seq, 2 * hps * LANES), BF16)],
        compiler_params=pltpu.CompilerParams(
            dimension_semantics=("arbitrary", "arbitrary", "arbitrary"),
            vmem_limit_bytes=VMEM_LIMIT),
        name="attn_prompt",
    )(q, qx, k, kx, v).reshape(batch * seq, D_ATTN)


def _suffix_sum_kernel(x_ref, o_ref):
    x = x_ref[...]
    n = x.shape[1]
    lane = lax.broadcasted_iota(jnp.int32, x.shape, 1)
    s = x
    d = 1
    while d < n:
        s = s + jnp.where(lane < n - d, pltpu.roll(s, n - d, axis=1), 0.0)
        d *= 2
    o_ref[...] = s - x


def _suffix_sum(x):
    return pl.pallas_call(
        _suffix_sum_kernel,
        out_shape=jax.ShapeDtypeStruct(x.shape, F32),
        compiler_params=pltpu.CompilerParams(vmem_limit_bytes=VMEM_LIMIT),
        name="cache_decay",
    )(x)


def _attn_sample_kernel(q_ref, ck_ref, cv_ref, r_ref, kn_ref, vn_ref, rn_ref, o_ref,
                        m_ref, l_ref, acc_ref, *, nq):
    j = pl.program_id(1)
    heads = range(N_HEADS)
    rows = [slice(h * nq, (h + 1) * nq) for h in heads]
    lanes = [slice(h * HEAD_DIM, (h + 1) * HEAD_DIM) for h in heads]
    q = q_ref[0]

    def stack(parts):
        return jnp.concatenate(parts, axis=0)

    def update(scores, decay, pv, mask=None):
        n = decay.shape[1]
        s = stack(scores) + stack([jnp.broadcast_to(decay[h:h + 1, :], (nq, n)) for h in heads])
        if mask is not None:
            s = jnp.where(mask, s, NEG)
        m_old = m_ref[:, 0:1]
        m_new = jnp.maximum(m_old, jnp.max(s, axis=-1, keepdims=True))
        alpha = jnp.exp(m_old - m_new)
        p = jnp.exp(s - m_new)
        l_new = alpha * l_ref[:, 0:1] + jnp.sum(p, axis=-1, keepdims=True)
        p = p.astype(BF16)
        acc_ref[...] = alpha * acc_ref[...] + stack([pv(h, p[rows[h]]) for h in heads])
        m_ref[...] = jnp.broadcast_to(m_new, m_ref.shape)
        l_ref[...] = jnp.broadcast_to(l_new, l_ref.shape)

    @pl.when(j == 0)
    def _():
        m_ref[...] = jnp.full(m_ref.shape, NEG, F32)
        l_ref[...] = jnp.zeros_like(l_ref)
        acc_ref[...] = jnp.zeros_like(acc_ref)
        q_id = lax.broadcasted_iota(jnp.int32, (N_HEADS * nq, nq), 0) % nq
        k_id = lax.broadcasted_iota(jnp.int32, (N_HEADS * nq, nq), 1)
        kn = kn_ref[0]
        vn = vn_ref[0]
        update([_dot_nt(q[:, lanes[h]], kn[:, lanes[h]]) for h in heads], rn_ref[0],
               lambda h, p: _dot(p, vn[:, lanes[h]]), mask=k_id <= q_id)

    update([_dot(q[:, lanes[h]], ck_ref[0, h].astype(BF16)) for h in heads], r_ref[0],
           lambda h, p: _dot_nt(p, cv_ref[0, h].astype(BF16)))

    @pl.when(j == pl.num_programs(1) - 1)
    def _():
        o = (acc_ref[...] / l_ref[:, 0:1]).astype(BF16)
        for h in heads:
            o_ref[0, :, lanes[h]] = o[rows[h]]


def _attn_sample(qh, cache_kt, cache_vt, r_cache, k_new, v_new, r_new, *, tk):
    nb, _, _, past = cache_kt.shape
    nq = k_new.shape[1]
    cache_spec = pl.BlockSpec((1, N_HEADS, HEAD_DIM, tk), lambda b, j: (b, 0, 0, j))
    return pl.pallas_call(
        functools.partial(_attn_sample_kernel, nq=nq),
        grid=(nb, past // tk),
        in_specs=[pl.BlockSpec((1, nq, D_ATTN), lambda b, j: (b, 0, 0)),
                  cache_spec, cache_spec,
                  pl.BlockSpec((1, N_HEADS, tk), lambda b, j: (b, 0, j)),
                  pl.BlockSpec((1, nq, D_ATTN), lambda b, j: (b, 0, 0)),
                  pl.BlockSpec((1, nq, D_ATTN), lambda b, j: (b, 0, 0)),
                  pl.BlockSpec((1, N_HEADS, nq), lambda b, j: (b, 0, 0))],
        out_specs=pl.BlockSpec((1, nq, D_ATTN), lambda b, j: (b, 0, 0)),
        out_shape=jax.ShapeDtypeStruct((nb, nq, D_ATTN), BF16),
        scratch_shapes=[pltpu.VMEM((N_HEADS * nq, LANES), F32),
                        pltpu.VMEM((N_HEADS * nq, LANES), F32),
                        pltpu.VMEM((N_HEADS * nq, HEAD_DIM), F32)],
        compiler_params=pltpu.CompilerParams(
            dimension_semantics=("arbitrary", "arbitrary"), vmem_limit_bytes=VMEM_LIMIT),
        name="attn_sample",
    )(qh, cache_kt, cache_vt, r_cache, k_new, v_new, r_new)


def _rms(x, g):
    return x * lax.rsqrt(jnp.mean(x * x, axis=-1, keepdims=True) + RMS_EPS) * g


def _out_ffn_rows(x_ref, ao_ref, sga_ref, cyg_ref, wao_ref, wout_ref, g2_ref, wg_ref, wu_ref,
                  wd_ref, gf_ref, y_ref):
    ap = _dot(ao_ref[...], wao_ref[...])
    mixed = cyg_ref[...].astype(F32) + sga_ref[...].astype(F32) * ap
    h = x_ref[...] + _dot(mixed.astype(BF16), wout_ref[...])
    z = _rms(h, g2_ref[...]).astype(BF16)
    gate = _dot(z, wg_ref[...])
    act = (gate * jax.nn.sigmoid(gate) * _dot(z, wu_ref[...])).astype(BF16)
    h = h + _dot(act, wd_ref[...])
    y_ref[...] = _rms(h, gf_ref[...])


def _out_ffn_kernel(*refs, n_prompt):
    prompt_in, sample_in, consts = refs[0:4], refs[4:8], refs[8:15]
    y_prompt_ref, y_sample_ref = refs[15:]
    i = pl.program_id(0)

    @pl.when(i < n_prompt)
    def _():
        _out_ffn_rows(*prompt_in, *consts, y_prompt_ref)

    @pl.when(i == n_prompt)
    def _():
        _out_ffn_rows(*sample_in, *consts, y_sample_ref)


def _out_ffn(prompt, sample, w, *, R):
    n_prompt = prompt[0].shape[0] // R
    assert sample[0].shape[0] == R
    widths = (D_MODEL, D_ATTN, D_MODEL, D_MODEL)
    prompt_map = lambda i: (jnp.minimum(i, n_prompt - 1), 0)
    sample_map = lambda i: (0, 0)
    consts = [w["wao"], w["wout"], w["g2"], w["wg"], w["wu"], w["wd"], w["gf"]]
    in_specs = [pl.BlockSpec((R, n), prompt_map) for n in widths]
    in_specs += [pl.BlockSpec((R, n), sample_map, pipeline_mode=pl.Buffered(1)) for n in widths]
    in_specs += [_const_spec(c.shape) for c in consts]
    return pl.pallas_call(
        functools.partial(_out_ffn_kernel, n_prompt=n_prompt),
        grid=(n_prompt + 1,),
        in_specs=in_specs,
        out_specs=[pl.BlockSpec((R, D_MODEL), prompt_map), pl.BlockSpec((R, D_MODEL), sample_map)],
        out_shape=[jax.ShapeDtypeStruct((prompt[0].shape[0], D_MODEL), F32),
                   jax.ShapeDtypeStruct((R, D_MODEL), F32)],
        compiler_params=pltpu.CompilerParams(
            dimension_semantics=("arbitrary",), vmem_limit_bytes=VMEM_LIMIT),
        name="out_ffn",
    )(*prompt, *sample, *consts)


def _prep_weights(norm_mix_g, w_in, b_f, w_dw, b_dw, ln_g, ln_b, w_conv_pw, w_attn_o, w_out,
                  norm_ffn_g, w_gate, w_up, w_down, final_norm_g):
    o_q = 2 * D_CONV
    o_f = o_q + 3 * D_ATTN
    o_gc = o_f + N_HEADS
    o_ga = o_gc + D_MODEL
    row = lambda a: a.reshape(1, -1).astype(F32)
    return {
        "g1": row(norm_mix_g),
        "wglu": w_in[:, :o_q].astype(BF16),
        "wqkv": w_in[:, o_q:o_f].astype(BF16),
        "wf": jnp.pad(w_in[:, o_f:o_gc], ((0, 0), (0, LANES - N_HEADS))).astype(BF16),
        "bf": jnp.pad(row(b_f), ((0, 0), (0, LANES - N_HEADS))),
        "wgc": w_in[:, o_gc:o_ga].astype(BF16),
        "wga": w_in[:, o_ga:].astype(BF16),
        "wdw": jnp.pad(w_dw.astype(F32), ((0, HIST_PAD - CONV_WIDTH), (0, 0))),
        "bdw": row(b_dw), "lng": row(ln_g), "lnb": row(ln_b),
        "wpw": w_conv_pw.astype(BF16),
        "wao": w_attn_o.astype(BF16), "wout": w_out.astype(BF16), "g2": row(norm_ffn_g),
        "wg": w_gate.astype(BF16), "wu": w_up.astype(BF16), "wd": w_down.astype(BF16),
        "gf": row(final_norm_g),
    }


def kernel(x_prompt, x_sample, cache_k, cache_v, cache_logf, state_conv, norm_mix_g, w_in, b_f,
           w_dw, b_dw, ln_g, ln_b, w_conv_pw, w_attn_o, w_out, norm_ffn_g, w_gate, w_up, w_down,
           final_norm_g):
    B, S, _ = x_prompt.shape
    NB, T, _ = x_sample.shape
    P = cache_k.shape[2]
    w = _prep_weights(norm_mix_g[0], w_in[0], b_f[0], w_dw[0], b_dw[0], ln_g[0], ln_b[0],
                      w_conv_pw[0], w_attn_o[0], w_out[0], norm_ffn_g[0], w_gate[0], w_up[0],
                      w_down[0], final_norm_g)

    xp = x_prompt.reshape(B * S, D_MODEL)
    hist0 = jnp.zeros((B, HIST_PAD, D_CONV), F32)
    q, qx, kx, kb, vb, k5, v5, lf, _, sga, cyg, nh = _inproj(xp, hist0, w, nb=1, T=512, carry=True)
    ao = _attn_prompt(q, qx, kb, kx, vb, batch=B, seq=S, tq=1024, tk=1024, tkd=512, hps=2)
    prompt_rows = (xp, ao, sga, cyg)
    k_prompt = k5.reshape(1, B, S, N_HEADS, HEAD_DIM)
    v_prompt = v5.reshape(1, B, S, N_HEADS, HEAD_DIM)
    logf_prompt = jnp.transpose(lf, (0, 2, 1)).reshape(1, B, S, N_HEADS)
    conv_prompt = nh[:, HIST_PAD - HIST:, :].reshape(1, B, HIST, D_CONV)

    xs = x_sample.reshape(NB * T, D_MODEL)
    hist_s = jnp.pad(state_conv[0].astype(F32), ((0, 0), (HIST_PAD - HIST, 0), (0, 0)))
    q, _, _, kb, vb, k5, v5, lf, c, sga, cyg, nh = _inproj(xs, hist_s, w, nb=NB, T=T, carry=False)
    r_new = -jnp.transpose(c.reshape(N_HEADS, NB, T), (1, 0, 2))
    clf_t = jnp.transpose(cache_logf[0].astype(F32), (0, 2, 1)).reshape(NB * N_HEADS, P)
    r_cache = _suffix_sum(clf_t).reshape(NB, N_HEADS, P)
    ao = _attn_sample(q.reshape(NB, T, D_ATTN), jnp.transpose(cache_k[0], (0, 2, 3, 1)),
                      jnp.transpose(cache_v[0], (0, 2, 3, 1)), r_cache,
                      kb.reshape(NB, T, D_ATTN), vb.reshape(NB, T, D_ATTN), r_new, tk=P)
    y_prompt, y_sample = _out_ffn(prompt_rows, (xs, ao.reshape(NB * T, D_ATTN), sga, cyg), w,
                                  R=NB * T)
    y_prompt = y_prompt.reshape(B, S, D_MODEL)
    y_sample = y_sample.reshape(NB, T, D_MODEL)
    k_sample = k5.reshape(1, NB, T, N_HEADS, HEAD_DIM)
    v_sample = v5.reshape(1, NB, T, N_HEADS, HEAD_DIM)
    logf_sample = jnp.transpose(lf.reshape(N_HEADS, NB, T), (1, 2, 0)).reshape(1, NB, T, N_HEADS)
    conv_sample = nh[:, HIST_PAD - HIST:, :].reshape(1, NB, HIST, D_CONV)

    return (y_prompt, y_sample, k_prompt, v_prompt, logf_prompt, conv_prompt,
            k_sample, v_sample, logf_sample, conv_sample)
```

```python
import functools

import jax
import jax.numpy as jnp
import numpy as np
from jax import lax
from jax.experimental import pallas as pl
from jax.experimental.pallas import tpu as pltpu

D_MODEL = 1024
N_HEADS = 8
HEAD_DIM = 64
D_ATTN = N_HEADS * HEAD_DIM
D_CONV = D_MODEL // 2
CONV_WIDTH = 31
HIST = CONV_WIDTH - 1
D_FF = 2816
RMS_EPS = 1e-6
LN_EPS = 1e-5
NEG = -1e30
SCALE = HEAD_DIM ** -0.5

LANES = 128
HIST_PAD = 32
CONV_ROWS = 64
ROW_BLOCK = 128
VMEM_LIMIT = 56 * 1024 * 1024

F32 = jnp.float32
BF16 = jnp.bfloat16


def _dot(a, b):
    return jnp.dot(a, b, preferred_element_type=F32)


def _dot_nt(a, b):
    return lax.dot_general(a, b, (((1,), (1,)), ((), ())), preferred_element_type=F32)


def _const_spec(shape):
    return pl.BlockSpec(shape, lambda *_: (0,) * len(shape), pipeline_mode=pl.Buffered(1))


def _split3(c):
    hi = c.astype(BF16)
    r1 = c - hi.astype(F32)
    mid = r1.astype(BF16)
    lo = (r1 - mid.astype(F32)).astype(BF16)
    return hi, mid, lo


def _inproj_kernel(x_ref, hist_ref, g_ref, wglu_ref, wqkv_ref, wf_ref, bf_ref, wgc_ref, wga_ref,
                   wdw_ref, bdw_ref, lng_ref, lnb_ref, wpw_ref, pq_ref, pk_ref, cq_ref, ck_ref,
                   q_ref, qx_ref, kx_ref, kb_ref, vb_ref, k5_ref, v5_ref, lf_ref, c_ref, sga_ref,
                   cyg_ref, nh_ref, cpad_ref, act_ref, ccar_ref, sgc_ref, *, nb, T, carry):
    R = nb * T
    RB = min(T, CONV_ROWS)
    t = pl.program_id(1)

    xf = x_ref[...]
    ms = jnp.mean(xf * xf, axis=-1, keepdims=True)
    xn = (xf * lax.rsqrt(ms + RMS_EPS) * g_ref[...]).astype(BF16)

    glu = _dot(xn, wglu_ref[...])
    cu = glu[:, :D_CONV] * jax.nn.sigmoid(glu[:, D_CONV:])
    if carry:
        @pl.when(t == 0)
        def _():
            cpad_ref[:, 0:HIST_PAD, :] = hist_ref[...]
            ccar_ref[...] = jnp.zeros_like(ccar_ref)
    else:
        cpad_ref[:, 0:HIST_PAD, :] = hist_ref[...]
    for b in range(nb):
        cpad_ref[b, HIST_PAD:HIST_PAD + T, :] = cu[b * T:(b + 1) * T]

    def qkv_stage():
        qkv = _dot(xn, wqkv_ref[...])
        q_ref[...] = (qkv[:, :D_ATTN] * SCALE).astype(BF16)
        kf = qkv[:, D_ATTN:2 * D_ATTN]
        vf = qkv[:, 2 * D_ATTN:]
        kb_ref[...] = kf.astype(BF16)
        vb_ref[...] = vf.astype(BF16)
        for h in range(N_HEADS):
            hs = slice(h * HEAD_DIM, (h + 1) * HEAD_DIM)
            k5_ref[pl.ds(h, R, stride=N_HEADS), :] = kf[:, hs]
            v5_ref[pl.ds(h, R, stride=N_HEADS), :] = vf[:, hs]

    def attn_gate_stage():
        sga_ref[...] = jax.nn.sigmoid(_dot(xn, wga_ref[...])).astype(BF16)

    def conv_gate_stage():
        sgc_ref[...] = jax.nn.sigmoid(_dot(xn, wgc_ref[...]))

    def forget_stage():
        z = _dot(xn, wf_ref[...]) + bf_ref[...]
        lf = -(jnp.maximum(-z, 0.0) + jnp.log1p(jnp.exp(-jnp.abs(z))))
        lane = lax.broadcasted_iota(jnp.int32, (R, LANES), 1)
        lf = jnp.where(lane < N_HEADS, lf, 0.0)
        lf_ref[0] = lf.T[:N_HEADS]
        row = lax.broadcasted_iota(jnp.int32, (R, LANES), 0) % T
        c = lf
        d = 1
        while d < T:
            c = c + jnp.where(row >= d, pltpu.roll(c, d, axis=0), 0.0)
            d *= 2
        if carry:
            c = c + ccar_ref[0:1, :]
            ccar_ref[...] = jnp.broadcast_to(c[R - 1:R, :], ccar_ref.shape)
        c_ref[0] = c.T[:N_HEADS]
        hi, mid, lo = _split3(c)
        qx = _dot(hi, pq_ref[0]) + _dot(mid, pq_ref[1]) + _dot(lo, pq_ref[2]) + cq_ref[...]
        kx = _dot(hi, pk_ref[0]) + _dot(mid, pk_ref[1]) + _dot(lo, pk_ref[2]) + ck_ref[...]
        qx_ref[...] = qx.astype(BF16)
        kx_ref[...] = kx.astype(BF16)

    stages = [qkv_stage, attn_gate_stage, conv_gate_stage, forget_stage]
    blocks = [(b, r0) for b in range(nb) for r0 in range(0, T, RB)]
    per_stage = len(blocks) // len(stages)

    off = HIST_PAD - HIST
    win_rows = RB + HIST_PAD
    for n, (b, r0) in enumerate(blocks):
        tiles = []
        for lt in range(D_CONV // LANES):
            ls = slice(lt * LANES, (lt + 1) * LANES)
            win = cpad_ref[b, r0:r0 + win_rows, ls]
            acc = jnp.broadcast_to(bdw_ref[:, ls], (RB, LANES))
            for res in range(8):
                sh = win if res == 0 else pltpu.roll(win, win_rows - res, axis=0)
                for a in range(HIST_PAD // 8 + 1):
                    w = 8 * a + res - off
                    if 0 <= w < CONV_WIDTH:
                        acc = acc + sh[8 * a:8 * a + RB] * wdw_ref[w:w + 1, ls]
            tiles.append(acc)
        acc = jnp.concatenate(tiles, axis=1)
        mu = jnp.mean(acc, axis=-1, keepdims=True)
        dv = acc - mu
        var = jnp.mean(dv * dv, axis=-1, keepdims=True)
        y = dv * lax.rsqrt(var + LN_EPS) * lng_ref[...] + lnb_ref[...]
        act_ref[b * T + r0:b * T + r0 + RB, :] = (y * jax.nn.sigmoid(y)).astype(BF16)
        if n % per_stage == 0 and stages:
            stages.pop(0)()
    assert not stages

    nh_ref[...] = cpad_ref[:, T:T + HIST_PAD, :]
    if carry:
        cpad_ref[:, 0:HIST_PAD, :] = cpad_ref[:, T:T + HIST_PAD, :]

    cy = _dot(act_ref[...], wpw_ref[...])
    cyg_ref[...] = (sgc_ref[...] * cy).astype(BF16)


def _extra_lane(h):
    pair, odd = divmod(h, 2)
    return pair * LANES + (0 if odd else HEAD_DIM)


def _placement_constants():
    pq = np.zeros((3, LANES, D_ATTN), np.float32)
    pk = np.zeros((3, LANES, D_ATTN), np.float32)
    cq = np.zeros((1, D_ATTN), np.float32)
    ck = np.zeros((1, D_ATTN), np.float32)
    for h in range(N_HEADS):
        p = _extra_lane(h)
        for part in range(3):
            pq[part, h, p + part] = 1.0
            pk[part, h, p + 3 + part] = -1.0
            cq[0, p + 3 + part] = 1.0
            ck[0, p + part] = 1.0
    return pq, pk, cq, ck


def _inproj(x2d, hist, w, *, nb, T, carry):
    n_rows = x2d.shape[0]
    R = nb * T
    n_seg = hist.shape[0]
    n_outer = n_seg // nb
    n_t = n_rows // (R * n_outer)
    grid = (n_outer, n_t)
    row_map = lambda b, t: (b * n_t + t, 0)
    pq, pk, cq, ck = _placement_constants()

    def rows(width, dtype):
        return jax.ShapeDtypeStruct((n_rows, width), dtype), pl.BlockSpec((R, width), row_map)

    def head_rows():
        return (jax.ShapeDtypeStruct((n_rows * N_HEADS, HEAD_DIM), F32),
                pl.BlockSpec((R * N_HEADS, HEAD_DIM), row_map))

    def head_major():
        return (jax.ShapeDtypeStruct((n_outer, N_HEADS, n_t * R), F32),
                pl.BlockSpec((1, N_HEADS, R), lambda b, t: (b, 0, t)))

    outs = [rows(D_ATTN, BF16), rows(D_ATTN, BF16), rows(D_ATTN, BF16), rows(D_ATTN, BF16),
            rows(D_ATTN, BF16), head_rows(), head_rows(), head_major(), head_major(),
            rows(D_MODEL, BF16), rows(D_MODEL, BF16)]
    out_shape = [o[0] for o in outs] + [jax.ShapeDtypeStruct((n_seg, HIST_PAD, D_CONV), F32)]
    out_specs = [o[1] for o in outs] + [pl.BlockSpec((nb, HIST_PAD, D_CONV), lambda b, t: (b, 0, 0))]

    consts = [w["g1"], w["wglu"], w["wqkv"], w["wf"], w["bf"], w["wgc"], w["wga"], w["wdw"],
              w["bdw"], w["lng"], w["lnb"], w["wpw"], jnp.asarray(pq, BF16), jnp.asarray(pk, BF16),
              jnp.asarray(cq), jnp.asarray(ck)]
    in_specs = [pl.BlockSpec((R, D_MODEL), row_map),
                pl.BlockSpec((nb, HIST_PAD, D_CONV), lambda b, t: (b, 0, 0))]
    in_specs += [_const_spec(c.shape) for c in consts]

    return pl.pallas_call(
        functools.partial(_inproj_kernel, nb=nb, T=T, carry=carry),
        grid=grid,
        in_specs=in_specs,
        out_specs=out_specs,
        out_shape=out_shape,
        scratch_shapes=[pltpu.VMEM((nb, T + HIST_PAD, D_CONV), F32),
                        pltpu.VMEM((R, D_CONV), BF16),
                        pltpu.VMEM((8, LANES), F32),
                        pltpu.VMEM((R, D_MODEL), F32)],
        compiler_params=pltpu.CompilerParams(
            dimension_semantics=("arbitrary", "arbitrary"), vmem_limit_bytes=VMEM_LIMIT),
        name="inproj_conv",
    )(x2d, hist, *consts)


def _attn_prompt_kernel(q_ref, qx_ref, k_ref, kx_ref, v_ref, o_ref, ka_ref, va_ref, *, tq, tk, tkd, hps):
    i = pl.program_id(2)
    nh = 2 * hps
    lane = lax.broadcasted_iota(jnp.int32, (tq, LANES), 1)
    low = lane < HEAD_DIM
    row0 = pl.multiple_of(i * tq, tq)
    one_hi = (lane == HEAD_DIM).astype(BF16)
    one_lo = (lane == 0).astype(BF16)

    qa = []
    for hp in range(hps):
        ps = slice(hp * LANES, (hp + 1) * LANES)
        ev = slice(2 * hp * LANES, (2 * hp + 1) * LANES)
        od = slice((2 * hp + 1) * LANES, (2 * hp + 2) * LANES)
        q, qx, k, kx, v = (r[0, :, ps] for r in (q_ref, qx_ref, k_ref, kx_ref, v_ref))
        ka_ref[pl.ds(row0, tq), ev] = jnp.where(low, k, kx)
        ka_ref[pl.ds(row0, tq), od] = jnp.where(low, kx, k)
        va_ref[pl.ds(row0, tq), ev] = jnp.where(low, v, one_hi)
        va_ref[pl.ds(row0, tq), od] = jnp.where(low, one_lo, v)
        qa += [jnp.where(low, q, qx), jnp.where(low, qx, q)]

    def step(col0, width, carry, r_lo=None):
        masked = r_lo is not None
        r_lo = r_lo or 0
        out = []
        for e in range(nh):
            m, acc = carry[e]
            kj = ka_ref[pl.ds(col0, width), e * LANES:(e + 1) * LANES]
            vj = va_ref[pl.ds(col0, width), e * LANES:(e + 1) * LANES]
            s = _dot_nt(qa[e][r_lo:], kj)
            if masked:
                r_id = lax.broadcasted_iota(jnp.int32, (tq - r_lo, width), 0)
                c_id = lax.broadcasted_iota(jnp.int32, (tq - r_lo, width), 1)
                s = jnp.where(c_id <= r_id, s, NEG)
            m_new = jnp.maximum(m[r_lo:], jnp.max(s, axis=-1, keepdims=True))
            alpha = jnp.exp(m[r_lo:] - m_new)
            p = jnp.exp(s - m_new).astype(BF16)
            acc_new = alpha * acc[r_lo:] + _dot(p, vj)
            if r_lo:
                m_new = jnp.concatenate([m[:r_lo], m_new], axis=0)
                acc_new = jnp.concatenate([acc[:r_lo], acc_new], axis=0)
            out.append((m_new, acc_new))
        return tuple(out)

    init = tuple((jnp.full((tq, 1), NEG, F32), jnp.zeros((tq, LANES), F32)) for _ in range(nh))
    carry = lax.fori_loop(0, i * (tq // tk),
                          lambda j, c: step(pl.multiple_of(j * tk, tk), tk, c), init)
    for d in range(tq // tkd):
        carry = step(pl.multiple_of(row0 + d * tkd, tkd), tkd, carry, r_lo=d * tkd)
    for hp in range(hps):
        acc0, acc1 = carry[2 * hp][1], carry[2 * hp + 1][1]
        o0 = acc0 / acc0[:, HEAD_DIM:HEAD_DIM + 1]
        o1 = acc1 / acc1[:, 0:1]
        o_ref[0, :, hp * LANES:(hp + 1) * LANES] = jnp.where(low, o0, o1).astype(BF16)


def _attn_prompt(q, qx, k, kx, v, *, batch, seq, tq, tk, tkd, hps):
    shp = (batch, seq, D_ATTN)
    q, qx, k, kx, v = (a.reshape(shp) for a in (q, qx, k, kx, v))
    spec = pl.BlockSpec((1, tq, hps * LANES), lambda b, g, i: (b, i, g))
    return pl.pallas_call(
        functools.partial(_attn_prompt_kernel, tq=tq, tk=tk, tkd=tkd, hps=hps),
        grid=(batch, N_HEADS // (2 * hps), seq // tq),
        in_specs=[spec] * 5,
        out_specs=spec,
        out_shape=jax.ShapeDtypeStruct(shp, BF16),
        scratch_shapes=[pltpu.VMEM((seq, 2 * hps * LANES), BF16),
                        pltpu.VMEM((seq, 2 * hps * LANES), BF16)],
        compiler_params=pltpu.CompilerParams(
            dimension_semantics=("arbitrary", "arbitrary", "arbitrary"),
            vmem_limit_bytes=VMEM_LIMIT),
        name="attn_prompt",
    )(q, qx, k, kx, v).reshape(batch * seq, D_ATTN)


def _suffix_sum_kernel(x_ref, o_ref):
    x = x_ref[...]
    n = x.shape[1]
    lane = lax.broadcasted_iota(jnp.int32, x.shape, 1)
    s = x
    d = 1
    while d < n:
        s = s + jnp.where(lane < n - d, pltpu.roll(s, n - d, axis=1), 0.0)
        d *= 2
    o_ref[...] = s - x


def _suffix_sum(x, *, rows):
    spec = pl.BlockSpec((rows, x.shape[1]), lambda i: (i, 0))
    return pl.pallas_call(
        _suffix_sum_kernel,
        grid=(x.shape[0] // rows,),
        in_specs=[spec],
        out_specs=spec,
        out_shape=jax.ShapeDtypeStruct(x.shape, F32),
        compiler_params=pltpu.CompilerParams(
            dimension_semantics=("arbitrary",), vmem_limit_bytes=VMEM_LIMIT),
        name="cache_decay",
    )(x)


def _attn_sample_kernel(q_ref, ck_ref, cv_ref, r_ref, kn_ref, vn_ref, rn_ref, o_ref,
                        m_ref, l_ref, acc_ref, *, nq):
    j = pl.program_id(1)
    heads = range(N_HEADS)
    rows = [slice(h * nq, (h + 1) * nq) for h in heads]
    lanes = [slice(h * HEAD_DIM, (h + 1) * HEAD_DIM) for h in heads]
    q = q_ref[0]

    def stack(parts):
        return jnp.concatenate(parts, axis=0)

    def update(scores, decay, pv, mask=None):
        n = decay.shape[1]
        s = stack(scores) + stack([jnp.broadcast_to(decay[h:h + 1, :], (nq, n)) for h in heads])
        if mask is not None:
            s = jnp.where(mask, s, NEG)
        m_old = m_ref[:, 0:1]
        m_new = jnp.maximum(m_old, jnp.max(s, axis=-1, keepdims=True))
        alpha = jnp.exp(m_old - m_new)
        p = jnp.exp(s - m_new)
        l_new = alpha * l_ref[:, 0:1] + jnp.sum(p, axis=-1, keepdims=True)
        p = p.astype(BF16)
        acc_ref[...] = alpha * acc_ref[...] + stack([pv(h, p[rows[h]]) for h in heads])
        m_ref[...] = jnp.broadcast_to(m_new, m_ref.shape)
        l_ref[...] = jnp.broadcast_to(l_new, l_ref.shape)

    @pl.when(j == 0)
    def _():
        m_ref[...] = jnp.full(m_ref.shape, NEG, F32)
        l_ref[...] = jnp.zeros_like(l_ref)
        acc_ref[...] = jnp.zeros_like(acc_ref)
        q_id = lax.broadcasted_iota(jnp.int32, (N_HEADS * nq, nq), 0) % nq
        k_id = lax.broadcasted_iota(jnp.int32, (N_HEADS * nq, nq), 1)
        kn = kn_ref[0]
        vn = vn_ref[0]
        update([_dot_nt(q[:, lanes[h]], kn[:, lanes[h]]) for h in heads], rn_ref[0],
               lambda h, p: _dot(p, vn[:, lanes[h]]), mask=k_id <= q_id)

    update([_dot(q[:, lanes[h]], ck_ref[0, h].astype(BF16)) for h in heads], r_ref[0],
           lambda h, p: _dot_nt(p, cv_ref[0, h].astype(BF16)))

    @pl.when(j == pl.num_programs(1) - 1)
    def _():
        o = (acc_ref[...] / l_ref[:, 0:1]).astype(BF16)
        for h in heads:
            o_ref[0, :, lanes[h]] = o[rows[h]]


def _attn_sample(qh, cache_kt, cache_vt, r_cache, k_new, v_new, r_new, *, tk):
    nb, _, _, past = cache_kt.shape
    nq = k_new.shape[1]
    cache_spec = pl.BlockSpec((1, N_HEADS, HEAD_DIM, tk), lambda b, j: (b, 0, 0, j))
    return pl.pallas_call(
        functools.partial(_attn_sample_kernel, nq=nq),
        grid=(nb, past // tk),
        in_specs=[pl.BlockSpec((1, nq, D_ATTN), lambda b, j: (b, 0, 0)),
                  cache_spec, cache_spec,
                  pl.BlockSpec((1, N_HEADS, tk), lambda b, j: (b, 0, j)),
                  pl.BlockSpec((1, nq, D_ATTN), lambda b, j: (b, 0, 0)),
                  pl.BlockSpec((1, nq, D_ATTN), lambda b, j: (b, 0, 0)),
                  pl.BlockSpec((1, N_HEADS, nq), lambda b, j: (b, 0, 0))],
        out_specs=pl.BlockSpec((1, nq, D_ATTN), lambda b, j: (b, 0, 0)),
        out_shape=jax.ShapeDtypeStruct((nb, nq, D_ATTN), BF16),
        scratch_shapes=[pltpu.VMEM((N_HEADS * nq, LANES), F32),
                        pltpu.VMEM((N_HEADS * nq, LANES), F32),
                        pltpu.VMEM((N_HEADS * nq, HEAD_DIM), F32)],
        compiler_params=pltpu.CompilerParams(
            dimension_semantics=("arbitrary", "arbitrary"), vmem_limit_bytes=VMEM_LIMIT),
        name="attn_sample",
    )(qh, cache_kt, cache_vt, r_cache, k_new, v_new, r_new)


def _rms(x, g):
    return x * lax.rsqrt(jnp.mean(x * x, axis=-1, keepdims=True) + RMS_EPS) * g


def _out_ffn_rows(x_ref, ao_ref, sga_ref, cyg_ref, wao_ref, wout_ref, g2_ref, wg_ref, wu_ref,
                  wd_ref, gf_ref, y_ref):
    ap = _dot(ao_ref[...], wao_ref[...])
    mixed = cyg_ref[...].astype(F32) + sga_ref[...].astype(F32) * ap
    h = x_ref[...] + _dot(mixed.astype(BF16), wout_ref[...])
    z = _rms(h, g2_ref[...]).astype(BF16)
    gate = _dot(z, wg_ref[...])
    act = (gate * jax.nn.sigmoid(gate) * _dot(z, wu_ref[...])).astype(BF16)
    h = h + _dot(act, wd_ref[...])
    y_ref[...] = _rms(h, gf_ref[...])


def _out_ffn_kernel(*refs, n_prompt):
    prompt_in, sample_in, consts = refs[0:4], refs[4:8], refs[8:15]
    y_prompt_ref, y_sample_ref = refs[15:]
    i = pl.program_id(0)

    @pl.when(i < n_prompt)
    def _():
        _out_ffn_rows(*prompt_in, *consts, y_prompt_ref)

    @pl.when(i == n_prompt)
    def _():
        _out_ffn_rows(*sample_in, *consts, y_sample_ref)


def _out_ffn(prompt, sample, w, *, R):
    n_prompt = prompt[0].shape[0] // R
    assert sample[0].shape[0] == R
    widths = (D_MODEL, D_ATTN, D_MODEL, D_MODEL)
    prompt_map = lambda i: (jnp.minimum(i, n_prompt - 1), 0)
    sample_map = lambda i: (0, 0)
    consts = [w["wao"], w["wout"], w["g2"], w["wg"], w["wu"], w["wd"], w["gf"]]
    in_specs = [pl.BlockSpec((R, n), prompt_map) for n in widths]
    in_specs += [pl.BlockSpec((R, n), sample_map, pipeline_mode=pl.Buffered(1)) for n in widths]
    in_specs += [_const_spec(c.shape) for c in consts]
    return pl.pallas_call(
        functools.partial(_out_ffn_kernel, n_prompt=n_prompt),
        grid=(n_prompt + 1,),
        in_specs=in_specs,
        out_specs=[pl.BlockSpec((R, D_MODEL), prompt_map), pl.BlockSpec((R, D_MODEL), sample_map)],
        out_shape=[jax.ShapeDtypeStruct((prompt[0].shape[0], D_MODEL), F32),
                   jax.ShapeDtypeStruct((R, D_MODEL), F32)],
        compiler_params=pltpu.CompilerParams(
            dimension_semantics=("arbitrary",), vmem_limit_bytes=VMEM_LIMIT),
        name="out_ffn",
    )(*prompt, *sample, *consts)


def _prep_weights(norm_mix_g, w_in, b_f, w_dw, b_dw, ln_g, ln_b, w_conv_pw, w_attn_o, w_out,
                  norm_ffn_g, w_gate, w_up, w_down, final_norm_g):
    o_q = 2 * D_CONV
    o_f = o_q + 3 * D_ATTN
    o_gc = o_f + N_HEADS
    o_ga = o_gc + D_MODEL
    row = lambda a: a.reshape(1, -1).astype(F32)
    return {
        "g1": row(norm_mix_g),
        "wglu": w_in[:, :o_q].astype(BF16),
        "wqkv": w_in[:, o_q:o_f].astype(BF16),
        "wf": jnp.pad(w_in[:, o_f:o_gc], ((0, 0), (0, LANES - N_HEADS))).astype(BF16),
        "bf": jnp.pad(row(b_f), ((0, 0), (0, LANES - N_HEADS))),
        "wgc": w_in[:, o_gc:o_ga].astype(BF16),
        "wga": w_in[:, o_ga:].astype(BF16),
        "wdw": jnp.pad(w_dw.astype(F32), ((0, HIST_PAD - CONV_WIDTH), (0, 0))),
        "bdw": row(b_dw), "lng": row(ln_g), "lnb": row(ln_b),
        "wpw": w_conv_pw.astype(BF16),
        "wao": w_attn_o.astype(BF16), "wout": w_out.astype(BF16), "g2": row(norm_ffn_g),
        "wg": w_gate.astype(BF16), "wu": w_up.astype(BF16), "wd": w_down.astype(BF16),
        "gf": row(final_norm_g),
    }


def kernel(x_prompt, x_sample, cache_k, cache_v, cache_logf, state_conv, norm_mix_g, w_in, b_f,
           w_dw, b_dw, ln_g, ln_b, w_conv_pw, w_attn_o, w_out, norm_ffn_g, w_gate, w_up, w_down,
           final_norm_g):
    B, S, _ = x_prompt.shape
    NB, T, _ = x_sample.shape
    P = cache_k.shape[2]
    w = _prep_weights(norm_mix_g[0], w_in[0], b_f[0], w_dw[0], b_dw[0], ln_g[0], ln_b[0],
                      w_conv_pw[0], w_attn_o[0], w_out[0], norm_ffn_g[0], w_gate[0], w_up[0],
                      w_down[0], final_norm_g)

    xp = x_prompt.reshape(B * S, D_MODEL)
    hist0 = jnp.zeros((B, HIST_PAD, D_CONV), F32)
    q, qx, kx, kb, vb, k5, v5, lf, _, sga, cyg, nh = _inproj(xp, hist0, w, nb=1, T=512, carry=True)
    ao = _attn_prompt(q, qx, kb, kx, vb, batch=B, seq=S, tq=1024, tk=1024, tkd=512, hps=2)
    prompt_rows = (xp, ao, sga, cyg)
    k_prompt = k5.reshape(1, B, S, N_HEADS, HEAD_DIM)
    v_prompt = v5.reshape(1, B, S, N_HEADS, HEAD_DIM)
    logf_prompt = jnp.transpose(lf, (0, 2, 1)).reshape(1, B, S, N_HEADS)
    conv_prompt = nh[:, HIST_PAD - HIST:, :].reshape(1, B, HIST, D_CONV)

    xs = x_sample.reshape(NB * T, D_MODEL)
    hist_s = jnp.pad(state_conv[0].astype(F32), ((0, 0), (HIST_PAD - HIST, 0), (0, 0)))
    q, _, _, kb, vb, k5, v5, lf, c, sga, cyg, nh = _inproj(xs, hist_s, w, nb=NB // 2, T=T,
                                                           carry=False)
    per_stream = lambda a: jnp.transpose(a.reshape(2, N_HEADS, NB // 2, T), (0, 2, 1, 3)).reshape(
        NB, N_HEADS, T)
    r_new = -per_stream(c)
    clf_t = jnp.transpose(cache_logf[0].astype(F32), (0, 2, 1)).reshape(NB * N_HEADS, P)
    r_cache = _suffix_sum(clf_t, rows=4 * N_HEADS).reshape(NB, N_HEADS, P)
    ao = _attn_sample(q.reshape(NB, T, D_ATTN), jnp.transpose(cache_k[0], (0, 2, 3, 1)),
                      jnp.transpose(cache_v[0], (0, 2, 3, 1)), r_cache,
                      kb.reshape(NB, T, D_ATTN), vb.reshape(NB, T, D_ATTN), r_new, tk=P)
    y_prompt, y_sample = _out_ffn(prompt_rows, (xs, ao.reshape(NB * T, D_ATTN), sga, cyg), w,
                                  R=NB * T)
    y_prompt = y_prompt.reshape(B, S, D_MODEL)
    y_sample = y_sample.reshape(NB, T, D_MODEL)
    k_sample = k5.reshape(1, NB, T, N_HEADS, HEAD_DIM)
    v_sample = v5.reshape(1, NB, T, N_HEADS, HEAD_DIM)
    logf_sample = jnp.transpose(per_stream(lf), (0, 2, 1)).reshape(1, NB, T, N_HEADS)
    conv_sample = nh[:, HIST_PAD - HIST:, :].reshape(1, NB, HIST, D_CONV)

    return (y_prompt, y_sample, k_prompt, v_prompt, logf_prompt, conv_prompt,
            k_sample, v_sample, logf_sample, conv_sample)
```

```python
import functools

import jax
import jax.numpy as jnp
import numpy as np
from jax import lax
from jax.experimental import pallas as pl
from jax.experimental.pallas import tpu as pltpu

D_MODEL = 1024
N_HEADS = 8
HEAD_DIM = 64
D_ATTN = N_HEADS * HEAD_DIM
D_CONV = D_MODEL // 2
CONV_WIDTH = 31
HIST = CONV_WIDTH - 1
D_FF = 2816
RMS_EPS = 1e-6
LN_EPS = 1e-5
NEG = -1e30
SCALE = HEAD_DIM ** -0.5

LANES = 128
HIST_PAD = 32
CONV_ROWS = 64
VMEM_LIMIT = 56 * 1024 * 1024

ROW_TILE = 512
ATTN_Q_TILE = 1024
ATTN_KV_TILE = 1024
ATTN_DIAG_TILE = 512
ATTN_HEAD_PAIRS = 2

F32 = jnp.float32
BF16 = jnp.bfloat16


def _dot(a, b):
    return jnp.dot(a, b, preferred_element_type=F32)


def _dot_nt(a, b):
    return lax.dot_general(a, b, (((1,), (1,)), ((), ())), preferred_element_type=F32)


def _const_spec(shape):
    return pl.BlockSpec(shape, lambda *_: (0,) * len(shape), pipeline_mode=pl.Buffered(1))


def _split3(c):
    hi = c.astype(BF16)
    r1 = c - hi.astype(F32)
    mid = r1.astype(BF16)
    lo = (r1 - mid.astype(F32)).astype(BF16)
    return hi, mid, lo


def _inproj_kernel(x_ref, hist_ref, g_ref, wglu_ref, wqkv_ref, wf_ref, bf_ref, wgc_ref, wga_ref,
                   wdw_ref, bdw_ref, lng_ref, lnb_ref, wpw_ref, pq_ref, pk_ref, cq_ref, ck_ref,
                   q_ref, qx_ref, kx_ref, kb_ref, vb_ref, k5_ref, v5_ref, lf_ref, c_ref, sga_ref,
                   cyg_ref, nh_ref, cpad_ref, act_ref, ccar_ref, sgc_ref, *, nb, T, carry):
    R = nb * T
    RB = min(T, CONV_ROWS)
    t = pl.program_id(1)

    xf = x_ref[...]
    ms = jnp.mean(xf * xf, axis=-1, keepdims=True)
    xn = (xf * lax.rsqrt(ms + RMS_EPS) * g_ref[...]).astype(BF16)

    glu = _dot(xn, wglu_ref[...])
    cu = glu[:, :D_CONV] * jax.nn.sigmoid(glu[:, D_CONV:])
    if carry:
        @pl.when(t == 0)
        def _():
            cpad_ref[:, 0:HIST_PAD, :] = hist_ref[...]
            ccar_ref[...] = jnp.zeros_like(ccar_ref)
    else:
        cpad_ref[:, 0:HIST_PAD, :] = hist_ref[...]
    for b in range(nb):
        cpad_ref[b, HIST_PAD:HIST_PAD + T, :] = cu[b * T:(b + 1) * T]

    def qkv_stage():
        qkv = _dot(xn, wqkv_ref[...])
        q_ref[...] = (qkv[:, :D_ATTN] * SCALE).astype(BF16)
        kf = qkv[:, D_ATTN:2 * D_ATTN]
        vf = qkv[:, 2 * D_ATTN:]
        kb_ref[...] = kf.astype(BF16)
        vb_ref[...] = vf.astype(BF16)
        for h in range(N_HEADS):
            hs = slice(h * HEAD_DIM, (h + 1) * HEAD_DIM)
            k5_ref[pl.ds(h, R, stride=N_HEADS), :] = kf[:, hs]
            v5_ref[pl.ds(h, R, stride=N_HEADS), :] = vf[:, hs]

    def attn_gate_stage():
        sga_ref[...] = jax.nn.sigmoid(_dot(xn, wga_ref[...])).astype(BF16)

    def conv_gate_stage():
        sgc_ref[...] = jax.nn.sigmoid(_dot(xn, wgc_ref[...]))

    def forget_stage():
        z = _dot(xn, wf_ref[...]) + bf_ref[...]
        lf = -(jnp.maximum(-z, 0.0) + jnp.log1p(jnp.exp(-jnp.abs(z))))
        lane = lax.broadcasted_iota(jnp.int32, (R, LANES), 1)
        lf = jnp.where(lane < N_HEADS, lf, 0.0)
        lf_ref[0] = lf.T[:N_HEADS]
        row = lax.broadcasted_iota(jnp.int32, (R, LANES), 0) % T
        c = lf
        d = 1
        while d < T:
            c = c + jnp.where(row >= d, pltpu.roll(c, d, axis=0), 0.0)
            d *= 2
        if carry:
            c = c + ccar_ref[0:1, :]
            ccar_ref[...] = jnp.broadcast_to(c[R - 1:R, :], ccar_ref.shape)
        c_ref[0] = c.T[:N_HEADS]
        hi, mid, lo = _split3(c)
        qx = _dot(hi, pq_ref[0]) + _dot(mid, pq_ref[1]) + _dot(lo, pq_ref[2]) + cq_ref[...]
        kx = _dot(hi, pk_ref[0]) + _dot(mid, pk_ref[1]) + _dot(lo, pk_ref[2]) + ck_ref[...]
        qx_ref[...] = qx.astype(BF16)
        kx_ref[...] = kx.astype(BF16)

    stages = [qkv_stage, attn_gate_stage, conv_gate_stage, forget_stage]
    blocks = [(b, r0) for b in range(nb) for r0 in range(0, T, RB)]
    per_stage = len(blocks) // len(stages)

    off = HIST_PAD - HIST
    win_rows = RB + HIST_PAD
    for n, (b, r0) in enumerate(blocks):
        tiles = []
        for lt in range(D_CONV // LANES):
            ls = slice(lt * LANES, (lt + 1) * LANES)
            win = cpad_ref[b, r0:r0 + win_rows, ls]
            acc = jnp.broadcast_to(bdw_ref[:, ls], (RB, LANES))
            for res in range(8):
                sh = win if res == 0 else pltpu.roll(win, win_rows - res, axis=0)
                for a in range(HIST_PAD // 8 + 1):
                    w = 8 * a + res - off
                    if 0 <= w < CONV_WIDTH:
                        acc = acc + sh[8 * a:8 * a + RB] * wdw_ref[w:w + 1, ls]
            tiles.append(acc)
        acc = jnp.concatenate(tiles, axis=1)
        mu = jnp.mean(acc, axis=-1, keepdims=True)
        dv = acc - mu
        var = jnp.mean(dv * dv, axis=-1, keepdims=True)
        y = dv * lax.rsqrt(var + LN_EPS) * lng_ref[...] + lnb_ref[...]
        act_ref[b * T + r0:b * T + r0 + RB, :] = (y * jax.nn.sigmoid(y)).astype(BF16)
        if n % per_stage == 0 and stages:
            stages.pop(0)()
    assert not stages

    nh_ref[...] = cpad_ref[:, T:T + HIST_PAD, :]
    if carry:
        cpad_ref[:, 0:HIST_PAD, :] = cpad_ref[:, T:T + HIST_PAD, :]

    cy = _dot(act_ref[...], wpw_ref[...])
    cyg_ref[...] = (sgc_ref[...] * cy).astype(BF16)


def _extra_lane(h):
    pair, odd = divmod(h, 2)
    return pair * LANES + (0 if odd else HEAD_DIM)


def _placement_constants():
    pq = np.zeros((3, LANES, D_ATTN), np.float32)
    pk = np.zeros((3, LANES, D_ATTN), np.float32)
    cq = np.zeros((1, D_ATTN), np.float32)
    ck = np.zeros((1, D_ATTN), np.float32)
    for h in range(N_HEADS):
        p = _extra_lane(h)
        for part in range(3):
            pq[part, h, p + part] = 1.0
            pk[part, h, p + 3 + part] = -1.0
            cq[0, p + 3 + part] = 1.0
            ck[0, p + part] = 1.0
    return pq, pk, cq, ck


def _inproj(x2d, hist, w, *, nb, T, carry):
    n_rows = x2d.shape[0]
    R = nb * T
    n_seg = hist.shape[0]
    n_outer = n_seg // nb
    n_t = n_rows // (R * n_outer)
    grid = (n_outer, n_t)
    row_map = lambda b, t: (b * n_t + t, 0)
    pq, pk, cq, ck = _placement_constants()

    def rows(width, dtype):
        return jax.ShapeDtypeStruct((n_rows, width), dtype), pl.BlockSpec((R, width), row_map)

    def head_rows():
        return (jax.ShapeDtypeStruct((n_rows * N_HEADS, HEAD_DIM), F32),
                pl.BlockSpec((R * N_HEADS, HEAD_DIM), row_map))

    def head_major():
        return (jax.ShapeDtypeStruct((n_outer, N_HEADS, n_t * R), F32),
                pl.BlockSpec((1, N_HEADS, R), lambda b, t: (b, 0, t)))

    outs = [rows(D_ATTN, BF16), rows(D_ATTN, BF16), rows(D_ATTN, BF16), rows(D_ATTN, BF16),
            rows(D_ATTN, BF16), head_rows(), head_rows(), head_major(), head_major(),
            rows(D_MODEL, BF16), rows(D_MODEL, BF16)]
    out_shape = [o[0] for o in outs] + [jax.ShapeDtypeStruct((n_seg, HIST_PAD, D_CONV), F32)]
    out_specs = [o[1] for o in outs] + [pl.BlockSpec((nb, HIST_PAD, D_CONV), lambda b, t: (b, 0, 0))]

    consts = [w["g1"], w["wglu"], w["wqkv"], w["wf"], w["bf"], w["wgc"], w["wga"], w["wdw"],
              w["bdw"], w["lng"], w["lnb"], w["wpw"], jnp.asarray(pq, BF16), jnp.asarray(pk, BF16),
              jnp.asarray(cq), jnp.asarray(ck)]
    in_specs = [pl.BlockSpec((R, D_MODEL), row_map),
                pl.BlockSpec((nb, HIST_PAD, D_CONV), lambda b, t: (b, 0, 0))]
    in_specs += [_const_spec(c.shape) for c in consts]

    return pl.pallas_call(
        functools.partial(_inproj_kernel, nb=nb, T=T, carry=carry),
        grid=grid,
        in_specs=in_specs,
        out_specs=out_specs,
        out_shape=out_shape,
        scratch_shapes=[pltpu.VMEM((nb, T + HIST_PAD, D_CONV), F32),
                        pltpu.VMEM((R, D_CONV), BF16),
                        pltpu.VMEM((8, LANES), F32),
                        pltpu.VMEM((R, D_MODEL), F32)],
        compiler_params=pltpu.CompilerParams(
            dimension_semantics=("arbitrary", "arbitrary"), vmem_limit_bytes=VMEM_LIMIT),
        name="inproj_conv",
    )(x2d, hist, *consts)


def _attn_prompt_kernel(q_ref, qx_ref, k_ref, kx_ref, v_ref, o_ref, ka_ref, va_ref, *, tq, tk, tkd, hps):
    i = pl.program_id(2)
    nh = 2 * hps
    lane = lax.broadcasted_iota(jnp.int32, (tq, LANES), 1)
    low = lane < HEAD_DIM
    row0 = pl.multiple_of(i * tq, tq)
    one_hi = (lane == HEAD_DIM).astype(BF16)
    one_lo = (lane == 0).astype(BF16)

    qa = []
    for hp in range(hps):
        ps = slice(hp * LANES, (hp + 1) * LANES)
        ev = slice(2 * hp * LANES, (2 * hp + 1) * LANES)
        od = slice((2 * hp + 1) * LANES, (2 * hp + 2) * LANES)
        q, qx, k, kx, v = (r[0, :, ps] for r in (q_ref, qx_ref, k_ref, kx_ref, v_ref))
        ka_ref[pl.ds(row0, tq), ev] = jnp.where(low, k, kx)
        ka_ref[pl.ds(row0, tq), od] = jnp.where(low, kx, k)
        va_ref[pl.ds(row0, tq), ev] = jnp.where(low, v, one_hi)
        va_ref[pl.ds(row0, tq), od] = jnp.where(low, one_lo, v)
        qa += [jnp.where(low, q, qx), jnp.where(low, qx, q)]

    def step(col0, width, carry, r_lo=None):
        masked = r_lo is not None
        r_lo = r_lo or 0
        out = []
        for e in range(nh):
            m, acc = carry[e]
            kj = ka_ref[pl.ds(col0, width), e * LANES:(e + 1) * LANES]
            vj = va_ref[pl.ds(col0, width), e * LANES:(e + 1) * LANES]
            s = _dot_nt(qa[e][r_lo:], kj)
            if masked:
                r_id = lax.broadcasted_iota(jnp.int32, (tq - r_lo, width), 0)
                c_id = lax.broadcasted_iota(jnp.int32, (tq - r_lo, width), 1)
                s = jnp.where(c_id <= r_id, s, NEG)
            m_new = jnp.maximum(m[r_lo:], jnp.max(s, axis=-1, keepdims=True))
            alpha = jnp.exp(m[r_lo:] - m_new)
            p = jnp.exp(s - m_new).astype(BF16)
            acc_new = alpha * acc[r_lo:] + _dot(p, vj)
            if r_lo:
                m_new = jnp.concatenate([m[:r_lo], m_new], axis=0)
                acc_new = jnp.concatenate([acc[:r_lo], acc_new], axis=0)
            out.append((m_new, acc_new))
        return tuple(out)

    init = tuple((jnp.full((tq, 1), NEG, F32), jnp.zeros((tq, LANES), F32)) for _ in range(nh))
    carry = lax.fori_loop(0, i * (tq // tk),
                          lambda j, c: step(pl.multiple_of(j * tk, tk), tk, c), init)
    for d in range(tq // tkd):
        carry = step(pl.multiple_of(row0 + d * tkd, tkd), tkd, carry, r_lo=d * tkd)
    for hp in range(hps):
        acc0, acc1 = carry[2 * hp][1], carry[2 * hp + 1][1]
        o0 = acc0 / acc0[:, HEAD_DIM:HEAD_DIM + 1]
        o1 = acc1 / acc1[:, 0:1]
        o_ref[0, :, hp * LANES:(hp + 1) * LANES] = jnp.where(low, o0, o1).astype(BF16)


def _attn_prompt(q, qx, k, kx, v, *, batch, seq, tq, tk, tkd, hps):
    shp = (batch, seq, D_ATTN)
    q, qx, k, kx, v = (a.reshape(shp) for a in (q, qx, k, kx, v))
    spec = pl.BlockSpec((1, tq, hps * LANES), lambda b, g, i: (b, i, g))
    return pl.pallas_call(
        functools.partial(_attn_prompt_kernel, tq=tq, tk=tk, tkd=tkd, hps=hps),
        grid=(batch, N_HEADS // (2 * hps), seq // tq),
        in_specs=[spec] * 5,
        out_specs=spec,
        out_shape=jax.ShapeDtypeStruct(shp, BF16),
        scratch_shapes=[pltpu.VMEM((seq, 2 * hps * LANES), BF16),
                        pltpu.VMEM((seq, 2 * hps * LANES), BF16)],
        compiler_params=pltpu.CompilerParams(
            dimension_semantics=("arbitrary", "arbitrary", "arbitrary"),
            vmem_limit_bytes=VMEM_LIMIT),
        name="attn_prompt",
    )(q, qx, k, kx, v).reshape(batch * seq, D_ATTN)


def _suffix_sum_kernel(x_ref, o_ref):
    x = x_ref[...]
    n = x.shape[1]
    lane = lax.broadcasted_iota(jnp.int32, x.shape, 1)
    s = x
    d = 1
    while d < n:
        s = s + jnp.where(lane < n - d, pltpu.roll(s, n - d, axis=1), 0.0)
        d *= 2
    o_ref[...] = s - x


def _suffix_sum(x):
    return pl.pallas_call(
        _suffix_sum_kernel,
        out_shape=jax.ShapeDtypeStruct(x.shape, F32),
        compiler_params=pltpu.CompilerParams(vmem_limit_bytes=VMEM_LIMIT),
        name="cache_decay",
    )(x)


def _attn_sample_kernel(q_ref, ck_ref, cv_ref, r_ref, kn_ref, vn_ref, rn_ref, o_ref,
                        m_ref, l_ref, acc_ref, *, nq):
    j = pl.program_id(1)
    heads = range(N_HEADS)
    rows = [slice(h * nq, (h + 1) * nq) for h in heads]
    lanes = [slice(h * HEAD_DIM, (h + 1) * HEAD_DIM) for h in heads]
    q = q_ref[0]

    def stack(parts):
        return jnp.concatenate(parts, axis=0)

    def update(scores, decay, pv, mask=None):
        n = decay.shape[1]
        s = stack(scores) + stack([jnp.broadcast_to(decay[h:h + 1, :], (nq, n)) for h in heads])
        if mask is not None:
            s = jnp.where(mask, s, NEG)
        m_old = m_ref[:, 0:1]
        m_new = jnp.maximum(m_old, jnp.max(s, axis=-1, keepdims=True))
        alpha = jnp.exp(m_old - m_new)
        p = jnp.exp(s - m_new)
        l_new = alpha * l_ref[:, 0:1] + jnp.sum(p, axis=-1, keepdims=True)
        p = p.astype(BF16)
        acc_ref[...] = alpha * acc_ref[...] + stack([pv(h, p[rows[h]]) for h in heads])
        m_ref[...] = jnp.broadcast_to(m_new, m_ref.shape)
        l_ref[...] = jnp.broadcast_to(l_new, l_ref.shape)

    @pl.when(j == 0)
    def _():
        m_ref[...] = jnp.full(m_ref.shape, NEG, F32)
        l_ref[...] = jnp.zeros_like(l_ref)
        acc_ref[...] = jnp.zeros_like(acc_ref)
        q_id = lax.broadcasted_iota(jnp.int32, (N_HEADS * nq, nq), 0) % nq
        k_id = lax.broadcasted_iota(jnp.int32, (N_HEADS * nq, nq), 1)
        kn = kn_ref[0]
        vn = vn_ref[0]
        update([_dot_nt(q[:, lanes[h]], kn[:, lanes[h]]) for h in heads], rn_ref[0],
               lambda h, p: _dot(p, vn[:, lanes[h]]), mask=k_id <= q_id)

    update([_dot(q[:, lanes[h]], ck_ref[0, h].astype(BF16)) for h in heads], r_ref[0],
           lambda h, p: _dot_nt(p, cv_ref[0, h].astype(BF16)))

    @pl.when(j == pl.num_programs(1) - 1)
    def _():
        o = (acc_ref[...] / l_ref[:, 0:1]).astype(BF16)
        for h in heads:
            o_ref[0, :, lanes[h]] = o[rows[h]]


def _attn_sample(qh, cache_kt, cache_vt, r_cache, k_new, v_new, r_new, *, tk):
    nb, _, _, past = cache_kt.shape
    nq = k_new.shape[1]
    cache_spec = pl.BlockSpec((1, N_HEADS, HEAD_DIM, tk), lambda b, j: (b, 0, 0, j))
    return pl.pallas_call(
        functools.partial(_attn_sample_kernel, nq=nq),
        grid=(nb, past // tk),
        in_specs=[pl.BlockSpec((1, nq, D_ATTN), lambda b, j: (b, 0, 0)),
                  cache_spec, cache_spec,
                  pl.BlockSpec((1, N_HEADS, tk), lambda b, j: (b, 0, j)),
                  pl.BlockSpec((1, nq, D_ATTN), lambda b, j: (b, 0, 0)),
                  pl.BlockSpec((1, nq, D_ATTN), lambda b, j: (b, 0, 0)),
                  pl.BlockSpec((1, N_HEADS, nq), lambda b, j: (b, 0, 0))],
        out_specs=pl.BlockSpec((1, nq, D_ATTN), lambda b, j: (b, 0, 0)),
        out_shape=jax.ShapeDtypeStruct((nb, nq, D_ATTN), BF16),
        scratch_shapes=[pltpu.VMEM((N_HEADS * nq, LANES), F32),
                        pltpu.VMEM((N_HEADS * nq, LANES), F32),
                        pltpu.VMEM((N_HEADS * nq, HEAD_DIM), F32)],
        compiler_params=pltpu.CompilerParams(
            dimension_semantics=("arbitrary", "arbitrary"), vmem_limit_bytes=VMEM_LIMIT),
        name="attn_sample",
    )(qh, cache_kt, cache_vt, r_cache, k_new, v_new, r_new)


def _rms(x, g):
    return x * lax.rsqrt(jnp.mean(x * x, axis=-1, keepdims=True) + RMS_EPS) * g


def _out_ffn_rows(x_ref, ao_ref, sga_ref, cyg_ref, wao_ref, wout_ref, g2_ref, wg_ref, wu_ref,
                  wd_ref, gf_ref, y_ref):
    ap = _dot(ao_ref[...], wao_ref[...])
    mixed = cyg_ref[...].astype(F32) + sga_ref[...].astype(F32) * ap
    h = x_ref[...] + _dot(mixed.astype(BF16), wout_ref[...])
    z = _rms(h, g2_ref[...]).astype(BF16)
    gate = _dot(z, wg_ref[...])
    act = (gate * jax.nn.sigmoid(gate) * _dot(z, wu_ref[...])).astype(BF16)
    h = h + _dot(act, wd_ref[...])
    y_ref[...] = _rms(h, gf_ref[...])


def _out_ffn_kernel(*refs, n_prompt):
    prompt_in, sample_in, consts = refs[0:4], refs[4:8], refs[8:15]
    y_prompt_ref, y_sample_ref = refs[15:]
    i = pl.program_id(0)

    @pl.when(i < n_prompt)
    def _():
        _out_ffn_rows(*prompt_in, *consts, y_prompt_ref)

    @pl.when(i == n_prompt)
    def _():
        _out_ffn_rows(*sample_in, *consts, y_sample_ref)


def _out_ffn(prompt, sample, w, *, R):
    n_prompt = prompt[0].shape[0] // R
    assert sample[0].shape[0] == R
    widths = (D_MODEL, D_ATTN, D_MODEL, D_MODEL)
    prompt_map = lambda i: (jnp.minimum(i, n_prompt - 1), 0)
    sample_map = lambda i: (0, 0)
    consts = [w["wao"], w["wout"], w["g2"], w["wg"], w["wu"], w["wd"], w["gf"]]
    in_specs = [pl.BlockSpec((R, n), prompt_map) for n in widths]
    in_specs += [pl.BlockSpec((R, n), sample_map, pipeline_mode=pl.Buffered(1)) for n in widths]
    in_specs += [_const_spec(c.shape) for c in consts]
    return pl.pallas_call(
        functools.partial(_out_ffn_kernel, n_prompt=n_prompt),
        grid=(n_prompt + 1,),
        in_specs=in_specs,
        out_specs=[pl.BlockSpec((R, D_MODEL), prompt_map), pl.BlockSpec((R, D_MODEL), sample_map)],
        out_shape=[jax.ShapeDtypeStruct((prompt[0].shape[0], D_MODEL), F32),
                   jax.ShapeDtypeStruct((R, D_MODEL), F32)],
        compiler_params=pltpu.CompilerParams(
            dimension_semantics=("arbitrary",), vmem_limit_bytes=VMEM_LIMIT),
        name="out_ffn",
    )(*prompt, *sample, *consts)


def _prep_weights(norm_mix_g, w_in, b_f, w_dw, b_dw, ln_g, ln_b, w_conv_pw, w_attn_o, w_out,
                  norm_ffn_g, w_gate, w_up, w_down, final_norm_g):
    o_q = 2 * D_CONV
    o_f = o_q + 3 * D_ATTN
    o_gc = o_f + N_HEADS
    o_ga = o_gc + D_MODEL
    row = lambda a: a.reshape(1, -1).astype(F32)
    return {
        "g1": row(norm_mix_g),
        "wglu": w_in[:, :o_q].astype(BF16),
        "wqkv": w_in[:, o_q:o_f].astype(BF16),
        "wf": jnp.pad(w_in[:, o_f:o_gc], ((0, 0), (0, LANES - N_HEADS))).astype(BF16),
        "bf": jnp.pad(row(b_f), ((0, 0), (0, LANES - N_HEADS))),
        "wgc": w_in[:, o_gc:o_ga].astype(BF16),
        "wga": w_in[:, o_ga:].astype(BF16),
        "wdw": jnp.pad(w_dw.astype(F32), ((0, HIST_PAD - CONV_WIDTH), (0, 0))),
        "bdw": row(b_dw), "lng": row(ln_g), "lnb": row(ln_b),
        "wpw": w_conv_pw.astype(BF16),
        "wao": w_attn_o.astype(BF16), "wout": w_out.astype(BF16), "g2": row(norm_ffn_g),
        "wg": w_gate.astype(BF16), "wu": w_up.astype(BF16), "wd": w_down.astype(BF16),
        "gf": row(final_norm_g),
    }


def kernel(x_prompt, x_sample, cache_k, cache_v, cache_logf, state_conv, norm_mix_g, w_in, b_f,
           w_dw, b_dw, ln_g, ln_b, w_conv_pw, w_attn_o, w_out, norm_ffn_g, w_gate, w_up, w_down,
           final_norm_g):
    B, S, _ = x_prompt.shape
    NB, T, _ = x_sample.shape
    P = cache_k.shape[2]
    w = _prep_weights(norm_mix_g[0], w_in[0], b_f[0], w_dw[0], b_dw[0], ln_g[0], ln_b[0],
                      w_conv_pw[0], w_attn_o[0], w_out[0], norm_ffn_g[0], w_gate[0], w_up[0],
                      w_down[0], final_norm_g)

    xp = x_prompt.reshape(B * S, D_MODEL)
    hist0 = jnp.zeros((B, HIST_PAD, D_CONV), F32)
    q, qx, kx, kb, vb, k5, v5, lf, _, sga, cyg, nh = _inproj(xp, hist0, w, nb=1, T=ROW_TILE,
                                                             carry=True)
    ao = _attn_prompt(q, qx, kb, kx, vb, batch=B, seq=S, tq=ATTN_Q_TILE, tk=ATTN_KV_TILE,
                      tkd=ATTN_DIAG_TILE, hps=ATTN_HEAD_PAIRS)
    prompt_rows = (xp, ao, sga, cyg)
    k_prompt = k5.reshape(1, B, S, N_HEADS, HEAD_DIM)
    v_prompt = v5.reshape(1, B, S, N_HEADS, HEAD_DIM)
    logf_prompt = jnp.transpose(lf, (0, 2, 1)).reshape(1, B, S, N_HEADS)
    conv_prompt = nh[:, HIST_PAD - HIST:, :].reshape(1, B, HIST, D_CONV)

    xs = x_sample.reshape(NB * T, D_MODEL)
    hist_s = jnp.pad(state_conv[0].astype(F32), ((0, 0), (HIST_PAD - HIST, 0), (0, 0)))
    q, _, _, kb, vb, k5, v5, lf, c, sga, cyg, nh = _inproj(xs, hist_s, w, nb=NB, T=T, carry=False)
    r_new = -jnp.transpose(c.reshape(N_HEADS, NB, T), (1, 0, 2))
    clf_t = jnp.transpose(cache_logf[0].astype(F32), (0, 2, 1)).reshape(NB * N_HEADS, P)
    r_cache = _suffix_sum(clf_t).reshape(NB, N_HEADS, P)
    ao = _attn_sample(q.reshape(NB, T, D_ATTN), jnp.transpose(cache_k[0], (0, 2, 3, 1)),
                      jnp.transpose(cache_v[0], (0, 2, 3, 1)), r_cache,
                      kb.reshape(NB, T, D_ATTN), vb.reshape(NB, T, D_ATTN), r_new, tk=P)
    y_prompt, y_sample = _out_ffn(prompt_rows, (xs, ao.reshape(NB * T, D_ATTN), sga, cyg), w,
                                  R=ROW_TILE)
    y_prompt = y_prompt.reshape(B, S, D_MODEL)
    y_sample = y_sample.reshape(NB, T, D_MODEL)
    k_sample = k5.reshape(1, NB, T, N_HEADS, HEAD_DIM)
    v_sample = v5.reshape(1, NB, T, N_HEADS, HEAD_DIM)
    logf_sample = jnp.transpose(lf.reshape(N_HEADS, NB, T), (1, 2, 0)).reshape(1, NB, T, N_HEADS)
    conv_sample = nh[:, HIST_PAD - HIST:, :].reshape(1, NB, HIST, D_CONV)

    return (y_prompt, y_sample, k_prompt, v_prompt, logf_prompt, conv_prompt,
            k_sample, v_sample, logf_sample, conv_sample)
```
